```python
import math
import jax, jax.numpy as jnp
from jax import lax
import numpy as np

D_MODEL = 4096
BATCH = 2
SEQ = 8192
DEPTH = 2

GRID_W = 64
CTX_LEN = 256
EPS = 1e-6
A_HEADS = 16
A_HEAD_DIM = 128
A_WIDTH = A_HEADS * 2 * A_HEAD_DIM
Q_BLOCK = 128
ROPE_THETA = 10000.0
M_HEADS = 8
M_QK_DIM = 256
M_V_DIM = 512
M_QK_WIDTH = M_HEADS * M_QK_DIM
M_WIDTH = M_HEADS * M_V_DIM
M_CHUNK = 128
M_CONV = 3
GATE_CAP = 15.0
PART_SIZES = (A_WIDTH, A_WIDTH, A_WIDTH, M_QK_WIDTH, M_QK_WIDTH, M_WIDTH, M_WIDTH, 4 * M_HEADS, D_MODEL, D_MODEL)
N_IN = 3 * A_WIDTH + 2 * M_QK_WIDTH + 2 * M_WIDTH + 4 * M_HEADS + 2 * D_MODEL
N_EXPERTS = 16
N_GROUPS = 4
EXPERTS_PER_GROUP = N_EXPERTS // N_GROUPS
TOPK_GROUP = 1
TOP_K = 2
D_FF = 768
MOE_BLOCK = 256

kernel_name = 'hybrid_diffattn_mlstm_groupmoe_dit'


def rms_norm(x, g):
    xf = x.astype(jnp.float32)
    y = xf * lax.rsqrt(jnp.mean(xf * xf, axis=-1, keepdims=True) + EPS)
    return (y * g.astype(jnp.float32)).astype(x.dtype)


def split_parts(p):
    return jnp.split(p, [int(o) for o in np.cumsum(PART_SIZES)[:-1]], axis=-1)


def conv_centred(u, w, b):
    k_w, s = w.shape[0], u.shape[1]
    up = jnp.pad(u, ((0, 0), (k_w // 2, k_w // 2), (0, 0)))
    out = b
    for j in range(k_w):
        out = out + up[:, j:j + s] * w[j]
    return out


def axial_rope_tables(n_rows):
    n_freq = A_HEAD_DIM // 4
    inv_freq = ROPE_THETA ** (-jnp.arange(n_freq, dtype=jnp.float32) / n_freq)
    ang_row = jnp.arange(n_rows, dtype=jnp.float32)[:, None] * inv_freq
    ang_col = jnp.arange(GRID_W, dtype=jnp.float32)[:, None] * inv_freq
    ang = jnp.stack([jnp.broadcast_to(ang_row[:, None, :], (n_rows, GRID_W, n_freq)),
                     jnp.broadcast_to(ang_col[None, :, :], (n_rows, GRID_W, n_freq))], axis=2)
    ang = ang.reshape(n_rows * GRID_W, 2, n_freq)
    return jnp.cos(ang), jnp.sin(ang)


def apply_rope(t, cos, sin):
    b, s, h, m, dh = t.shape
    n_freq = dh // 4
    tf = t.astype(jnp.float32).reshape(b, s, h, m, 2, 2, n_freq)
    t1, t2 = tf[..., 0, :], tf[..., 1, :]
    cs, sn = cos[None, :, None, None], sin[None, :, None, None]
    out = jnp.stack([t1 * cs - t2 * sn, t2 * cs + t1 * sn], axis=-2)
    return out.reshape(t.shape).astype(t.dtype)


def diff_attention(qb, k, v, lam):
    s = jnp.einsum('bqhmd,bkhmd->bhmqk', qb, k).astype(jnp.float32) * (A_HEAD_DIM ** -0.5)
    p = jax.nn.softmax(s, axis=-1)
    a = p[:, :, 0] - lam * p[:, :, 1]
    return jnp.einsum('bhqk,bkhe->bqhe', a.astype(v.dtype), v)


def mlstm_scan(q, k, v, li, lf, state):
    b_, h_, s_, _ = q.shape
    dv = v.shape[-1]
    nc = s_ // M_CHUNK

    def to_chunks(t):
        return jnp.moveaxis(t.reshape(t.shape[:2] + (nc, M_CHUNK) + t.shape[3:]), 2, 0)

    lower = jnp.tril(jnp.ones((M_CHUNK, M_CHUNK), dtype=bool))

    def step(carry, inp):
        c_st, n_st, m_st = carry
        qc, kc, vc, lic, lfc = inp
        bsum = jnp.cumsum(lfc, axis=-1)
        d_log = jnp.where(lower, bsum[..., :, None] - bsum[..., None, :] + lic[..., None, :], -jnp.inf)
        inter = bsum + m_st[..., None]
        m_pos = jnp.maximum(jnp.max(d_log, axis=-1), inter)
        w_intra = jnp.exp(d_log - m_pos[..., None])
        w_inter = jnp.exp(inter - m_pos)
        sc = jnp.einsum('bhld,bhrd->bhlr', qc, kc) * w_intra
        num = jnp.einsum('bhlr,bhrv->bhlv', sc, vc) + w_inter[..., None] * jnp.einsum('bhld,bhdv->bhlv', qc, c_st)
        den = jnp.sum(sc, axis=-1) + w_inter * jnp.einsum('bhld,bhd->bhl', qc, n_st)
        hc = num / jnp.maximum(jnp.abs(den), jnp.exp(-m_pos))[..., None]
        b_end = bsum[..., -1]
        w_log = b_end[..., None] - bsum + lic
        m_new = jnp.maximum(b_end + m_st, jnp.max(w_log, axis=-1))
        decay = jnp.exp(b_end + m_st - m_new)
        wk = jnp.exp(w_log - m_new[..., None])[..., None] * kc
        c_new = decay[..., None, None] * c_st + jnp.einsum('bhld,bhlv->bhdv', wk, vc)
        n_new = decay[..., None] * n_st + jnp.sum(wk, axis=2)
        return (c_new, n_new, m_new), hc

    state, hs = lax.scan(step, state, tuple(to_chunks(t) for t in (q, k, v, li, lf)))
    return state, jnp.moveaxis(hs, 0, 2).reshape(b_, h_, s_, dv)


def mlstm_inputs(parts, conv_w, conv_b, gate_b):
    mq, mk, mv, mg = parts[3], parts[4], parts[5], parts[7]
    b_, s_, _ = mq.shape
    qk = jax.nn.silu(conv_centred(jnp.concatenate([mq, mk], axis=-1), conv_w, conv_b)).astype(jnp.float32)

    def heads(t, dh):
        return t.reshape(b_, s_, M_HEADS, dh).transpose(0, 2, 1, 3)

    q = heads(qk[..., :M_QK_WIDTH], M_QK_DIM)
    k = heads(qk[..., M_QK_WIDTH:], M_QK_DIM) * (M_QK_DIM ** -0.5)
    v = heads(mv.astype(jnp.float32), M_V_DIM)
    g = mg.astype(jnp.float32).reshape(b_, s_, 4, M_HEADS) + gate_b.astype(jnp.float32)
    g = GATE_CAP * jnp.tanh(g / GATE_CAP)
    g = g.transpose(2, 0, 3, 1)
    return q, k, v, g[0::2], jax.nn.log_sigmoid(g[1::2])


def bidir_mlstm(m_c, m_x, need_ctx):
    qc, kc, vc, lic, lfc = m_c
    qx, kx, vx, lix, lfx = m_x
    b_ = qc.shape[0]
    h_x, h_c = 0.0, 0.0
    for d in range(2):
        if d == 0:
            fl = lambda t: t
        else:
            fl = lambda t: jnp.flip(t, axis=2)
        state0 = (jnp.zeros((b_, M_HEADS, M_QK_DIM, M_V_DIM), jnp.float32),
                  jnp.zeros((b_, M_HEADS, M_QK_DIM), jnp.float32),
                  jnp.zeros((b_, M_HEADS), jnp.float32))
        st_c, hc_d = mlstm_scan(fl(qc), fl(kc), fl(vc), fl(lic[d]), fl(lfc[d]), state0)
        _, hx_d = mlstm_scan(fl(qx), fl(kx), fl(vx), fl(lix[d]), fl(lfx[d]), st_c)
        h_x = h_x + fl(hx_d)
        if need_ctx:
            h_c = h_c + fl(hc_d)
    return h_x, (h_c if need_ctx else None)


def token_mixer(n_x, n_c, w_in, q_norm_g, k_norm_g, lam_vec, attn_head_g, conv_w, conv_b, gate_b,
                m_head_g, w_br_attn, w_br_mlstm, w_out, lam_init, cos, sin, need_ctx):
    parts_x = split_parts(n_x @ w_in)
    parts_c = split_parts(n_c @ w_in)
    lv = lam_vec.astype(jnp.float32)
    lam = jnp.exp(jnp.sum(lv[0] * lv[1])) - jnp.exp(jnp.sum(lv[2] * lv[3])) + lam_init

    def attn_heads(parts):
        b_, s_, _ = parts[0].shape
        shp = (b_, s_, A_HEADS, 2, A_HEAD_DIM)
        return (rms_norm(parts[0].reshape(shp), q_norm_g), rms_norm(parts[1].reshape(shp), k_norm_g),
                parts[2].reshape(b_, s_, A_HEADS, 2 * A_HEAD_DIM))

    def attn_out(a):
        b_, s_ = a.shape[:2]
        return (rms_norm(a, attn_head_g) * (1.0 - lam_init)).reshape(b_, s_, A_WIDTH)

    def mlstm_out(h, o):
        b_, _, s_, _ = h.shape
        hn = rms_norm(h.transpose(0, 2, 1, 3), m_head_g).astype(o.dtype)
        return (hn * jax.nn.sigmoid(o.reshape(b_, s_, M_HEADS, M_V_DIM))).reshape(b_, s_, M_WIDTH)

    def merge(parts, a_flat, m_flat):
        y = jax.nn.sigmoid(parts[8]) * (a_flat @ w_br_attn) + jax.nn.sigmoid(parts[9]) * (m_flat @ w_br_mlstm)
        return y @ w_out

    q_x, k_x, v_x = attn_heads(parts_x)
    q_x, k_x = apply_rope(q_x, cos, sin), apply_rope(k_x, cos, sin)
    q_c, k_c, v_c = attn_heads(parts_c)
    k_all = jnp.concatenate([k_x, k_c], axis=1)
    v_all = jnp.concatenate([v_x, v_c], axis=1)
    b_, s_ = n_x.shape[:2]
    nb = s_ // Q_BLOCK
    q_blocks = jnp.moveaxis(q_x.reshape((b_, nb, Q_BLOCK) + q_x.shape[2:]), 1, 0)
    a_x = lax.map(lambda qb: diff_attention(qb, k_all, v_all, lam), q_blocks)
    a_x = jnp.moveaxis(a_x, 0, 1).reshape(b_, s_, A_HEADS, 2 * A_HEAD_DIM)
    m_c = mlstm_inputs(parts_c, conv_w, conv_b, gate_b)
    m_x = mlstm_inputs(parts_x, conv_w, conv_b, gate_b)
    h_x, h_c = bidir_mlstm(m_c, m_x, need_ctx)
    out_x = merge(parts_x, attn_out(a_x), mlstm_out(h_x, parts_x[6]))
    if need_ctx:
        a_c = diff_attention(q_c, k_c, v_c, lam)
        out_c = merge(parts_c, attn_out(a_c), mlstm_out(h_c, parts_c[6]))
    else:
        out_c = None
    return out_x, out_c


def grouped_moe(h, w_router, router_bias, w_gate, w_up, w_down):
    t_n, d_n = h.shape
    scores = jax.nn.sigmoid((h @ w_router).astype(jnp.float32))
    sel = scores + router_bias.astype(jnp.float32)
    grp_score = jnp.sum(lax.top_k(sel.reshape(t_n, N_GROUPS, EXPERTS_PER_GROUP), 2)[0], axis=-1)
    _, gidx = lax.top_k(grp_score, TOPK_GROUP)
    gmask = jnp.any(gidx[..., None] == jnp.arange(N_GROUPS), axis=-2)
    masked = jnp.where(jnp.repeat(gmask, EXPERTS_PER_GROUP, axis=-1), sel, -jnp.inf)
    _, eidx = lax.top_k(masked, TOP_K)
    wts = jnp.take_along_axis(scores, eidx, axis=-1)
    wts = wts / jnp.sum(wts, axis=-1, keepdims=True)
    n_asg = t_n * TOP_K
    e_flat = eidx.reshape(n_asg)
    tok_flat = jnp.repeat(jnp.arange(t_n, dtype=jnp.int32), TOP_K)
    w_flat = wts.reshape(n_asg)
    order = jnp.argsort(e_flat)
    e_sorted = e_flat[order]
    counts = jnp.bincount(e_flat, length=N_EXPERTS)
    padded = (counts + MOE_BLOCK - 1) // MOE_BLOCK * MOE_BLOCK
    starts = jnp.cumsum(counts) - counts
    pad_end = jnp.cumsum(padded)
    pad_start = pad_end - padded
    dest = pad_start[e_sorted] + jnp.arange(n_asg) - starts[e_sorted]
    n_pad = n_asg + N_EXPERTS * MOE_BLOCK
    n_blk = n_pad // MOE_BLOCK
    buf_tok = jnp.full((n_pad,), t_n, jnp.int32).at[dest].set(tok_flat[order])
    buf_w = jnp.zeros((n_pad,), jnp.float32).at[dest].set(w_flat[order])
    blk_e = jnp.minimum(jnp.searchsorted(pad_end, jnp.arange(n_blk) * MOE_BLOCK, side='right'), N_EXPERTS - 1)
    h_pad = jnp.concatenate([h, jnp.zeros((1, d_n), h.dtype)], axis=0)

    def expert_block(args):
        tok, wt, e = args
        xb = h_pad[tok]
        y = (jax.nn.silu(xb @ w_gate[e]) * (xb @ w_up[e])) @ w_down[e]
        return y * wt[:, None].astype(y.dtype)

    ys = lax.map(expert_block, (buf_tok.reshape(n_blk, MOE_BLOCK), buf_w.reshape(n_blk, MOE_BLOCK), blk_e))
    return jnp.zeros((t_n + 1, d_n), ys.dtype).at[buf_tok].add(ys.reshape(n_pad, d_n))[:t_n]


def setup_inputs(seed: int = 0) -> dict:
    key = jax.random.key(seed)
    ks = jax.random.split(key, 25)
    D = D_MODEL

    def nrm(k, shape, s):
        return jax.random.normal(k, shape, jnp.float32) * s

    gate_offset = jnp.array([0.0, 3.0, 0.0, 3.0], jnp.float32)[None, :, None]
    return {
        'x': nrm(ks[0], (BATCH, SEQ, D), 1.0),
        'c': nrm(ks[1], (BATCH, D), 1.0),
        'ctx': nrm(ks[2], (BATCH, CTX_LEN, D), 1.0),
        'c_ctx': nrm(ks[3], (D,), 1.0),
        'w_ada': nrm(ks[4], (DEPTH, D, 6 * D), 0.5 * D ** -0.5),
        'b_ada': nrm(ks[5], (DEPTH, 6 * D), 0.02),
        'g_norm1': 1.0 + nrm(ks[6], (DEPTH, D), 0.02),
        'g_norm2': 1.0 + nrm(ks[7], (DEPTH, D), 0.02),
        'w_in': nrm(ks[8], (DEPTH, D, N_IN), D ** -0.5),
        'q_norm_g': 1.0 + nrm(ks[9], (DEPTH, A_HEAD_DIM), 0.02),
        'k_norm_g': 1.0 + nrm(ks[10], (DEPTH, A_HEAD_DIM), 0.02),
        'diff_lambda': nrm(ks[11], (DEPTH, 4, A_HEAD_DIM), 0.1),
        'attn_head_g': 1.0 + nrm(ks[12], (DEPTH, 2 * A_HEAD_DIM), 0.02),
        'conv_w': nrm(ks[13], (DEPTH, M_CONV, 2 * M_QK_WIDTH), M_CONV ** -0.5),
        'conv_b': nrm(ks[14], (DEPTH, 2 * M_QK_WIDTH), 0.02),
        'mlstm_gate_b': gate_offset + nrm(ks[15], (DEPTH, 4, M_HEADS), 0.5),
        'mlstm_head_g': 1.0 + nrm(ks[16], (DEPTH, M_HEADS, M_V_DIM), 0.02),
        'w_br_attn': nrm(ks[17], (DEPTH, A_WIDTH, D), A_WIDTH ** -0.5),
        'w_br_mlstm': nrm(ks[18], (DEPTH, M_WIDTH, D), M_WIDTH ** -0.5),
        'w_out': nrm(ks[19], (DEPTH, D, D), D ** -0.5),
        'w_router': nrm(ks[20], (D, N_EXPERTS), D ** -0.5),
        'router_bias': nrm(ks[21], (N_EXPERTS,), 0.01),
        'w_gate': nrm(ks[22], (DEPTH, N_EXPERTS, D, D_FF), D ** -0.5),
        'w_up': nrm(ks[23], (DEPTH, N_EXPERTS, D, D_FF), D ** -0.5),
        'w_down': nrm(ks[24], (DEPTH, N_EXPERTS, D_FF, D), D_FF ** -0.5),
    }


def reference(x, c, ctx, c_ctx, w_ada, b_ada, g_norm1, g_norm2, w_in, q_norm_g, k_norm_g, diff_lambda,
              attn_head_g, conv_w, conv_b, mlstm_gate_b, mlstm_head_g, w_br_attn, w_br_mlstm, w_out,
              w_router, router_bias, w_gate, w_up, w_down):
    B, S, D = x.shape
    L_ctx = ctx.shape[1]
    ROWS = S // GRID_W
    cos, sin = axial_rope_tables(ROWS)
    h_x, h_c = x, ctx
    for l in range(DEPTH):
        need_ctx = l < DEPTH - 1
        lam_init = 0.8 - 0.6 * math.exp(-0.3 * l)
        mod_x = jnp.split((jax.nn.silu(c) @ w_ada[l] + b_ada[l])[:, None, :], 6, axis=-1)
        mod_c = jnp.split((jax.nn.silu(c_ctx) @ w_ada[l] + b_ada[l])[None, None, :], 6, axis=-1)
        n_x = rms_norm(h_x, g_norm1[l]) * (1.0 + mod_x[1]) + mod_x[0]
        n_c = rms_norm(h_c, g_norm1[l]) * (1.0 + mod_c[1]) + mod_c[0]
        y_x, y_c = token_mixer(n_x, n_c, w_in[l], q_norm_g[l], k_norm_g[l], diff_lambda[l], attn_head_g[l],
                               conv_w[l], conv_b[l], mlstm_gate_b[l], mlstm_head_g[l], w_br_attn[l],
                               w_br_mlstm[l], w_out[l], lam_init, cos, sin, need_ctx)
        h_x = h_x + mod_x[2] * y_x
        n2_x = rms_norm(h_x, g_norm2[l]) * (1.0 + mod_x[4]) + mod_x[3]
        if need_ctx:
            h_c = h_c + mod_c[2] * y_c
            n2_c = rms_norm(h_c, g_norm2[l]) * (1.0 + mod_c[4]) + mod_c[3]
            f = grouped_moe(jnp.concatenate([n2_x.reshape(B * S, D), n2_c.reshape(B * L_ctx, D)], axis=0),
                            w_router, router_bias, w_gate[l], w_up[l], w_down[l])
            h_c = h_c + mod_c[5] * f[B * S:].reshape(B, L_ctx, D)
            f_x = f[:B * S]
        else:
            f_x = grouped_moe(n2_x.reshape(B * S, D), w_router, router_bias, w_gate[l], w_up[l], w_down[l])
        h_x = h_x + mod_x[5] * f_x.reshape(B, S, D)
    return h_x
```

```python
import functools
import math

import jax
import jax.numpy as jnp
from jax import lax
from jax.experimental import pallas as pl
from jax.experimental.pallas import tpu as pltpu

GRID_W = 64
EPS = 1e-6
ROPE_THETA = 10000.0
M_CHUNK = 128
M_CONV = 3
GATE_CAP = 15.0
N_GROUPS = 4
GROUP_SIZE = 4
N_PAIRS = N_GROUPS * 6
NEG = -1e30

V7X_VMEM_LIMIT = 56 * 1024 * 1024
ROW_TILE = 256
MOE_BLOCK = 128
BF16_ROWS = 16

f32 = jnp.float32
bf16 = jnp.bfloat16


def _params(*sem):
    return pltpu.CompilerParams(dimension_semantics=sem, vmem_limit_bytes=V7X_VMEM_LIMIT)


def _sigmoid(x):
    return 1.0 / (1.0 + jnp.exp(-x))


def _ada_kernel(cv_ref, w_ref, b_ref, o_ref):
    cv = cv_ref[...]
    s = (cv * _sigmoid(cv)).astype(bf16)
    o_ref[0] = jnp.dot(s, w_ref[0].astype(bf16), preferred_element_type=f32) + b_ref[0]


def _ada(cv, w_ada, b_ada):
    depth, d, n6 = w_ada.shape
    tn = min(512, n6)
    return pl.pallas_call(
        _ada_kernel,
        out_shape=jax.ShapeDtypeStruct((depth, 8, n6), f32),
        grid=(depth, n6 // tn),
        in_specs=[
            pl.BlockSpec((8, d), lambda l, j: (0, 0)),
            pl.BlockSpec((1, d, tn), lambda l, j: (l, 0, j)),
            pl.BlockSpec((1, 1, tn), lambda l, j: (l, 0, j)),
        ],
        out_specs=pl.BlockSpec((1, 8, tn), lambda l, j: (l, 0, j)),
        compiler_params=_params("arbitrary", "arbitrary"),
        name="ada",
    )(cv, w_ada, b_ada.reshape(depth, 1, n6))


def _norm_kernel(*refs, has_resid, has_norm):
    it = iter(refs)
    h_ref = next(it)
    if has_resid:
        f_ref, gate_ref = next(it), next(it)
    if has_norm:
        g_ref, shift_ref, scale_ref = next(it), next(it), next(it)
    h = h_ref[...]
    if has_resid:
        h = h + gate_ref[0] * f_ref[...]
        next(it)[...] = h
    if has_norm:
        ms = jnp.mean(h * h, axis=-1, keepdims=True)
        y = h * lax.rsqrt(ms + EPS) * g_ref[...]
        n_ref = next(it)
        n_ref[...] = (y * (1.0 + scale_ref[0]) + shift_ref[0]).astype(n_ref.dtype)


def _resid_norm(h, rows, seg_of, mods, resid=None, norm=None, n_dtype=bf16):
    t, d = h.shape
    tr = ROW_TILE
    row_spec = pl.BlockSpec((tr, d), lambda i: (i, 0))

    def mod_spec(layer, k):
        return pl.BlockSpec((1, 1, d), lambda i: ((layer * 8 + seg_of(i, tr)) * 6 + k, 0, 0))

    args, in_specs, out_shape, out_specs, aliases = [h], [row_spec], [], [], {}
    if resid is not None:
        f, layer, k_gate = resid
        args += [f, mods]
        in_specs += [row_spec, mod_spec(layer, k_gate)]
        out_shape.append(jax.ShapeDtypeStruct((t, d), f32))
        out_specs.append(row_spec)
        aliases = {0: 0}
    if norm is not None:
        g, layer, k_shift, k_scale = norm
        args += [g.reshape(1, d), mods, mods]
        in_specs += [pl.BlockSpec((1, d), lambda i: (0, 0)), mod_spec(layer, k_shift), mod_spec(layer, k_scale)]
        out_shape.append(jax.ShapeDtypeStruct((t, d), n_dtype))
        out_specs.append(row_spec)
    outs = pl.pallas_call(
        functools.partial(_norm_kernel, has_resid=resid is not None, has_norm=norm is not None),
        out_shape=out_shape,
        grid=(rows // tr,),
        in_specs=in_specs,
        out_specs=out_specs,
        input_output_aliases=aliases,
        compiler_params=_params("arbitrary"),
        name="resid_norm",
    )(*args)
    return outs


def _mm_kernel(a_ref, b_ref, o_ref):
    o_ref[...] = jnp.dot(a_ref[...], b_ref[...], preferred_element_type=f32).astype(o_ref.dtype)


def _matmul(a, b, out_dtype, tm=512, tn=1024):
    m, k = a.shape
    n = b.shape[1]
    tm, tn = min(tm, m), math.gcd(tn, n)
    assert m % tm == 0 and tn % 128 == 0
    return pl.pallas_call(
        _mm_kernel,
        out_shape=jax.ShapeDtypeStruct((m, n), out_dtype),
        grid=(n // tn, m // tm),
        in_specs=[pl.BlockSpec((tm, k), lambda j, i: (i, 0)), pl.BlockSpec((k, tn), lambda j, i: (0, j))],
        out_specs=pl.BlockSpec((tm, tn), lambda j, i: (i, j)),
        compiler_params=_params("arbitrary", "arbitrary"),
        name="in_proj",
    )(a, b)


def _gates_kernel(n_ref, w_ref, b_ref, g_ref, c_ref, *, mh):
    tr = n_ref.shape[0]
    mg = jnp.dot(n_ref[...], w_ref[...], preferred_element_type=f32) + b_ref[...]
    g = GATE_CAP * jnp.tanh(mg * (1.0 / GATE_CAP))
    lane = lax.broadcasted_iota(jnp.int32, g.shape, 1)
    is_forget = (lane // mh) % 2 == 1
    log_sig = jnp.minimum(g, 0.0) - jnp.log(1.0 + jnp.exp(-jnp.abs(g)))
    val = jnp.where(is_forget, log_sig, g)
    g_ref[...] = val
    r = lax.broadcasted_iota(jnp.int32, (M_CHUNK, M_CHUNK), 0)
    c = lax.broadcasted_iota(jnp.int32, (M_CHUNK, M_CHUNK), 1)
    lower = (c <= r).astype(f32)
    upper = (c >= r).astype(f32)
    fwd = lax.broadcasted_iota(jnp.int32, (M_CHUNK, g.shape[1]), 1) < 2 * mh
    for ch in range(tr // M_CHUNK):
        v = val[ch * M_CHUNK:(ch + 1) * M_CHUNK]
        cf = jnp.dot(lower, v, preferred_element_type=f32, precision=lax.Precision.HIGHEST)
        cb = jnp.dot(upper, v, preferred_element_type=f32, precision=lax.Precision.HIGHEST)
        c_ref[ch * M_CHUNK:(ch + 1) * M_CHUNK, :] = jnp.where(fwd, cf, cb)


def _gates(n, w_g, gate_b, mh):
    t, d = n.shape
    ng = 4 * mh
    tr = ROW_TILE
    spec = pl.BlockSpec((tr, ng), lambda i: (i, 0))
    return pl.pallas_call(
        functools.partial(_gates_kernel, mh=mh),
        out_shape=[jax.ShapeDtypeStruct((t, ng), f32)] * 2,
        grid=(t // tr,),
        in_specs=[pl.BlockSpec((tr, d), lambda i: (i, 0)), pl.BlockSpec((d, ng), lambda i: (0, 0)),
                  pl.BlockSpec((1, ng), lambda i: (0, 0))],
        out_specs=[spec, spec],
        compiler_params=_params("arbitrary"),
        name="gates",
    )(n, w_g, gate_b.reshape(1, ng).astype(f32))


def _qk_kernel(x_ref, g_ref, cos_ref, sin_ref, o_ref, *, hd):
    x = x_ref[...].astype(f32)
    cos, sin = cos_ref[...], sin_ref[...]
    lane = lax.broadcasted_iota(jnp.int32, cos.shape, 1)
    first_half = (lane % (hd // 2)) < hd // 4
    for s in range(x.shape[1] // hd):
        sl = slice(s * hd, (s + 1) * hd)
        xs = x[:, sl]
        ms = jnp.mean(xs * xs, axis=-1, keepdims=True)
        y = xs * lax.rsqrt(ms + EPS) * g_ref[:, sl]
        partner = jnp.where(first_half, pltpu.roll(y, hd - hd // 4, 1), pltpu.roll(y, hd // 4, 1))
        o_ref[:, sl] = (y * cos + partner * sin).astype(o_ref.dtype)


def _qk_prep(parts, gain, cos_t, sin_t, width, n_x_tiles, tiles_per_seq, hd):
    t = parts.shape[0]
    tr = ROW_TILE
    wq = min(512, width)

    def tbl(i, j):
        return (jnp.where(i < n_x_tiles, i % tiles_per_seq, tiles_per_seq), 0)

    return pl.pallas_call(
        functools.partial(_qk_kernel, hd=hd),
        out_shape=jax.ShapeDtypeStruct((t, width), bf16),
        grid=(t // tr, width // wq),
        in_specs=[pl.BlockSpec((tr, wq), lambda i, j: (i, j)), pl.BlockSpec((1, wq), lambda i, j: (0, j)),
                  pl.BlockSpec((tr, hd), tbl), pl.BlockSpec((tr, hd), tbl)],
        out_specs=pl.BlockSpec((tr, wq), lambda i, j: (i, j)),
        compiler_params=_params("arbitrary", "arbitrary"),
        name="qk_prep",
    )(parts, gain, cos_t, sin_t)


def _attn_kernel(q_ref, kx_ref, kc_ref, vx_ref, vc_ref, lam_ref, g_ref, o_ref, m_ref, l_ref, acc_ref,
                 *, hd, tk, n_xchunks, nq_x):
    i = pl.program_id(2)
    q = q_ref[...]
    m_ref[...] = jnp.full(m_ref.shape, NEG, f32)
    l_ref[...] = jnp.zeros(l_ref.shape, f32)
    acc_ref[...] = jnp.zeros(acc_ref.shape, f32)

    def chunk(k, v):
        for mp in range(2):
            s = lax.dot_general(q[:, mp * hd:(mp + 1) * hd], k[:, mp * hd:(mp + 1) * hd],
                                (((1,), (1,)), ((), ())), preferred_element_type=f32)
            m_old = m_ref[mp]
            m_new = jnp.maximum(m_old, jnp.max(s, axis=-1, keepdims=True))
            alpha = jnp.exp(m_old - m_new)
            p = jnp.exp(s - m_new)
            l_ref[mp] = alpha * l_ref[mp] + jnp.sum(p, axis=-1, keepdims=True)
            acc_ref[mp] = alpha * acc_ref[mp] + jnp.dot(p.astype(bf16), v, preferred_element_type=f32)
            m_ref[mp] = m_new

    chunk(kc_ref[...], vc_ref[...])

    def body(c, carry):
        off = pl.multiple_of(c * tk, tk)
        chunk(kx_ref[pl.ds(off, tk), :], vx_ref[pl.ds(off, tk), :])
        return carry

    lax.fori_loop(0, jnp.where(i < nq_x, n_xchunks, 0), body, 0)
    a = acc_ref[0] / l_ref[0] - lam_ref[...] * (acc_ref[1] / l_ref[1])
    ms = jnp.mean(a * a, axis=-1, keepdims=True)
    o_ref[...] = (a * lax.rsqrt(ms + EPS) * g_ref[...]).astype(o_ref.dtype)


def _attention(qk2, parts, lam, out_gain, dims, with_ctx_queries):
    b, s, l, heads, hd = dims
    a_w = heads * 2 * hd
    t = qk2.shape[0]
    tq = ROW_TILE
    tk = min(512, s)
    nq_x, nq_c = s // tq, l // tq
    nq = nq_x + (nq_c if with_ctx_queries else 0)
    hw = 2 * hd
    k_col, v_col = a_w // hw, 2 * a_w // hw
    ctx_row = b * s // l

    def q_idx(bi, h, i):
        return (jnp.where(i < nq_x, bi * nq_x + i, b * nq_x + bi * nq_c + (i - nq_x)), h)

    return pl.pallas_call(
        functools.partial(_attn_kernel, hd=hd, tk=tk, n_xchunks=s // tk, nq_x=nq_x),
        out_shape=jax.ShapeDtypeStruct((t if with_ctx_queries else b * s, a_w), bf16),
        grid=(b, heads, nq),
        in_specs=[
            pl.BlockSpec((tq, hw), q_idx),
            pl.BlockSpec((s, hw), lambda bi, h, i: (bi, k_col + h)),
            pl.BlockSpec((l, hw), lambda bi, h, i: (ctx_row + bi, k_col + h)),
            pl.BlockSpec((s, hw), lambda bi, h, i: (bi, v_col + h)),
            pl.BlockSpec((l, hw), lambda bi, h, i: (ctx_row + bi, v_col + h)),
            pl.BlockSpec((1, 1), lambda bi, h, i: (0, 0)),
            pl.BlockSpec((1, hw), lambda bi, h, i: (0, 0)),
        ],
        out_specs=pl.BlockSpec((tq, hw), q_idx),
        scratch_shapes=[pltpu.VMEM((2, tq, 1), f32), pltpu.VMEM((2, tq, 1), f32), pltpu.VMEM((2, tq, hw), f32)],
        compiler_params=_params("arbitrary", "arbitrary", "arbitrary"),
        name="diff_attn",
    )(qk2, qk2, qk2, parts, parts, lam, out_gain)


def _mprep_kernel(x_ref, p_ref, n_ref, w_ref, b_ref, sc_ref, o_ref, *, s, l, bs):
    tr = x_ref.shape[0]
    row0 = pl.program_id(0) * tr
    in_x = row0 < bs
    rel = jnp.where(in_x, row0 % s, (row0 - bs) % l)
    seq_len = jnp.where(in_x, s, l)
    keep_prev = jnp.where(rel == 0, 0.0, 1.0)
    keep_next = jnp.where(rel + tr == seq_len, 0.0, 1.0)
    x = x_ref[...].astype(f32)
    prev = p_ref[...].astype(f32)[BF16_ROWS - 1:BF16_ROWS] * keep_prev
    nxt = n_ref[...].astype(f32)[0:1] * keep_next
    rows = lax.broadcasted_iota(jnp.int32, x.shape, 0)
    xm = jnp.where(rows == 0, prev, pltpu.roll(x, 1, 0))
    xp = jnp.where(rows == tr - 1, nxt, pltpu.roll(x, tr - 1, 0))
    w = w_ref[...]
    u = b_ref[...] + xm * w[0:1] + x * w[1:2] + xp * w[2:3]
    o_ref[...] = (u * _sigmoid(u) * sc_ref[...]).astype(o_ref.dtype)


def _mlstm_prep(parts, col0, conv_w, conv_b, post, dims):
    b, s, l = dims
    t = parts.shape[0]
    width = conv_w.shape[1]
    tr = ROW_TILE
    wc = min(512, width)
    cb = col0 // wc
    sub = tr // BF16_ROWS
    last = t // BF16_ROWS - 1
    return pl.pallas_call(
        functools.partial(_mprep_kernel, s=s, l=l, bs=b * s),
        out_shape=jax.ShapeDtypeStruct((t, width), bf16),
        grid=(t // tr, width // wc),
        in_specs=[
            pl.BlockSpec((tr, wc), lambda i, j: (i, cb + j)),
            pl.BlockSpec((BF16_ROWS, wc), lambda i, j: (jnp.maximum(i * sub - 1, 0), cb + j)),
            pl.BlockSpec((BF16_ROWS, wc), lambda i, j: (jnp.minimum((i + 1) * sub, last), cb + j)),
            pl.BlockSpec((M_CONV, wc), lambda i, j: (0, j)),
            pl.BlockSpec((1, wc), lambda i, j: (0, j)),
            pl.BlockSpec((1, wc), lambda i, j: (0, j)),
        ],
        out_specs=pl.BlockSpec((tr, wc), lambda i, j: (i, j)),
        compiler_params=_params("arbitrary", "arbitrary"),
        name="mlstm_prep",
    )(parts, parts, parts, conv_w, conv_b, post)


def _scan_kernel(q_ref, k_ref, v_ref, gcol_ref, ccol_ref, grow_ref, crow_ref, o_ref, c_ref, n_ref, m_ref, *, mh):
    d = pl.program_id(0)
    h = pl.program_id(2)

    @pl.when(pl.program_id(3) == 0)
    def _():
        c_ref[...] = jnp.zeros(c_ref.shape, f32)
        n_ref[...] = jnp.zeros(n_ref.shape, f32)
        m_ref[...] = jnp.zeros(m_ref.shape, f32)

    col_i = d * 2 * mh + h
    col_f = col_i + mh
    lane = lax.broadcasted_iota(jnp.int32, gcol_ref.shape, 1)
    li_col = jnp.sum(jnp.where(lane == col_i, gcol_ref[...], 0.0), axis=1, keepdims=True)
    bs_col = jnp.sum(jnp.where(lane == col_f, ccol_ref[...], 0.0), axis=1, keepdims=True)
    li_row = grow_ref[pl.ds(col_i, 1), :]
    bs_row = crow_ref[pl.ds(col_f, 1), :]
    m_st = m_ref[...]
    r = lax.broadcasted_iota(jnp.int32, (M_CHUNK, M_CHUNK), 0)
    c = lax.broadcasted_iota(jnp.int32, (M_CHUNK, M_CHUNK), 1)
    visible = (r - c) * (1 - 2 * d) >= 0
    last = jnp.where(d == 0, M_CHUNK - 1, 0)
    b_end = jnp.sum(jnp.where(c[0:1] == last, bs_row, 0.0), axis=1, keepdims=True)

    d_log = jnp.where(visible, bs_col + (li_row - bs_row), NEG)
    inter = bs_col + m_st
    m_pos = jnp.maximum(jnp.max(d_log, axis=1, keepdims=True), inter)
    w_intra = jnp.exp(d_log - m_pos)
    w_inter = jnp.exp(inter - m_pos)
    q, k, v = q_ref[...], k_ref[...], v_ref[...]
    sc = lax.dot_general(q, k, (((1,), (1,)), ((), ())), preferred_element_type=f32) * w_intra
    c_st = c_ref[...]
    num = jnp.dot(sc.astype(bf16), v, preferred_element_type=f32) + w_inter * jnp.dot(
        q, c_st.astype(bf16), preferred_element_type=f32)
    qn = jnp.sum(q.astype(f32) * n_ref[...], axis=1, keepdims=True)
    den = jnp.sum(sc, axis=1, keepdims=True) + w_inter * qn
    o_ref[0] = (num / jnp.maximum(jnp.abs(den), jnp.exp(-m_pos))).astype(o_ref.dtype)

    w_log = b_end - bs_col + li_col
    m_new = jnp.maximum(b_end + m_st, jnp.max(w_log, axis=0, keepdims=True))
    decay = jnp.exp(b_end + m_st - m_new)
    wk = jnp.exp(w_log - m_new) * k.astype(f32)
    c_ref[...] = decay * c_st + lax.dot_general(wk.astype(bf16), v, (((0,), (0,)), ((), ())),
                                                preferred_element_type=f32)
    n_ref[...] = decay * n_ref[...] + jnp.sum(wk, axis=0, keepdims=True)
    m_ref[...] = m_new


def _mlstm_scan(mqk, parts, v_col0, g_col, c_col, g_row, c_row, dims):
    b, s, l, mh, dk, dv = dims
    t = mqk.shape[0]
    n_lc, n_sc = l // M_CHUNK, s // M_CHUNK
    ctx0 = b * s // M_CHUNK
    vb = v_col0 // dv
    ng = 4 * mh

    def chunk(d, bi, st):
        in_ctx = st < n_lc
        local = jnp.where(in_ctx, st, st - n_lc)
        n_local = jnp.where(in_ctx, n_lc, n_sc)
        pos = jnp.where(d == 0, local, n_local - 1 - local)
        return jnp.where(in_ctx, ctx0 + bi * n_lc, bi * n_sc) + pos

    return pl.pallas_call(
        functools.partial(_scan_kernel, mh=mh),
        out_shape=jax.ShapeDtypeStruct((2, t, mh * dv), bf16),
        grid=(2, b, mh, n_lc + n_sc),
        in_specs=[
            pl.BlockSpec((M_CHUNK, dk), lambda d, bi, h, st: (chunk(d, bi, st), h)),
            pl.BlockSpec((M_CHUNK, dk), lambda d, bi, h, st: (chunk(d, bi, st), mh + h)),
            pl.BlockSpec((M_CHUNK, dv), lambda d, bi, h, st: (chunk(d, bi, st), vb + h)),
            pl.BlockSpec((M_CHUNK, ng), lambda d, bi, h, st: (chunk(d, bi, st), 0)),
            pl.BlockSpec((M_CHUNK, ng), lambda d, bi, h, st: (chunk(d, bi, st), 0)),
            pl.BlockSpec((ng, M_CHUNK), lambda d, bi, h, st: (0, chunk(d, bi, st))),
            pl.BlockSpec((ng, M_CHUNK), lambda d, bi, h, st: (0, chunk(d, bi, st))),
        ],
        out_specs=pl.BlockSpec((1, M_CHUNK, dv), lambda d, bi, h, st: (d, chunk(d, bi, st), h)),
        scratch_shapes=[pltpu.VMEM((dk, dv), f32), pltpu.VMEM((1, dk), f32), pltpu.VMEM((1, 1), f32)],
        compiler_params=_params("arbitrary", "arbitrary", "arbitrary", "arbitrary"),
        name="mlstm_scan",
    )(mqk, mqk, parts, g_col, c_col, g_row, c_row)


def _mout_kernel(h_ref, o_ref, g_ref, out_ref):
    hs = h_ref[0].astype(f32) + h_ref[1].astype(f32)
    ms = jnp.mean(hs * hs, axis=-1, keepdims=True)
    y = hs * lax.rsqrt(ms + EPS) * g_ref[0]
    out_ref[...] = (y * _sigmoid(o_ref[...].astype(f32))).astype(out_ref.dtype)


def _mlstm_out(hdir, parts, o_col0, head_g, rows):
    m_w = hdir.shape[2]
    mh, dv = head_g.shape
    tr = ROW_TILE
    ob = o_col0 // dv
    return pl.pallas_call(
        _mout_kernel,
        out_shape=jax.ShapeDtypeStruct((rows, m_w), bf16),
        grid=(rows // tr, mh),
        in_specs=[pl.BlockSpec((2, tr, dv), lambda i, h: (0, i, h)), pl.BlockSpec((tr, dv), lambda i, h: (i, ob + h)),
                  pl.BlockSpec((1, 1, dv), lambda i, h: (h, 0, 0))],
        out_specs=pl.BlockSpec((tr, dv), lambda i, h: (i, h)),
        compiler_params=_params("arbitrary", "arbitrary"),
        name="mlstm_out",
    )(hdir, parts, head_g.reshape(mh, 1, dv).astype(f32))


def _merge_kernel(a_ref, wa_ref, m_ref, wm_ref, ga_ref, gb_ref, o_ref):
    ya = jnp.dot(a_ref[...], wa_ref[...], preferred_element_type=f32)
    ym = jnp.dot(m_ref[...], wm_ref[...], preferred_element_type=f32)
    y = _sigmoid(ga_ref[...].astype(f32)) * ya + _sigmoid(gb_ref[...].astype(f32)) * ym
    o_ref[...] = y.astype(o_ref.dtype)


def _merge(a_flat, m_flat, w_a, w_m, parts, ga_col0, rows):
    d = w_a.shape[1]
    tm, tn = 512, min(512, d)
    ga, gb = ga_col0 // tn, (ga_col0 + d) // tn
    return pl.pallas_call(
        _merge_kernel,
        out_shape=jax.ShapeDtypeStruct((rows, d), bf16),
        grid=(d // tn, rows // tm),
        in_specs=[
            pl.BlockSpec((tm, a_flat.shape[1]), lambda j, i: (i, 0)),
            pl.BlockSpec((w_a.shape[0], tn), lambda j, i: (0, j)),
            pl.BlockSpec((tm, m_flat.shape[1]), lambda j, i: (i, 0)),
            pl.BlockSpec((w_m.shape[0], tn), lambda j, i: (0, j)),
            pl.BlockSpec((tm, tn), lambda j, i: (i, ga + j)),
            pl.BlockSpec((tm, tn), lambda j, i: (i, gb + j)),
        ],
        out_specs=pl.BlockSpec((tm, tn), lambda j, i: (i, j)),
        compiler_params=_params("arbitrary", "arbitrary"),
        name="branch_merge",
    )(a_flat, w_a, m_flat, w_m, parts, parts)


def _outproj_kernel(y_ref, w_ref, h_ref, gate_ref, o_ref):
    o_ref[...] = h_ref[...] + gate_ref[0] * jnp.dot(y_ref[...], w_ref[...], preferred_element_type=f32)


def _out_proj(y, w_out, h, mods, layer, seg_of, rows):
    t, d = h.shape
    tm, tn = 512, min(512, d)
    return pl.pallas_call(
        _outproj_kernel,
        out_shape=jax.ShapeDtypeStruct((t, d), f32),
        grid=(d // tn, rows // tm),
        in_specs=[
            pl.BlockSpec((tm, d), lambda j, i: (i, 0)),
            pl.BlockSpec((d, tn), lambda j, i: (0, j)),
            pl.BlockSpec((tm, tn), lambda j, i: (i, j)),
            pl.BlockSpec((1, 1, tn), lambda j, i: ((layer * 8 + seg_of(i, tm)) * 6 + 2, 0, j)),
        ],
        out_specs=pl.BlockSpec((tm, tn), lambda j, i: (i, j)),
        input_output_aliases={2: 0},
        compiler_params=_params("arbitrary", "arbitrary"),
        name="out_proj",
    )(y, w_out, h, mods)


def _route_kernel(h_ref, g_ref, shift_ref, scale_ref, wr_ref, bias_ref, n_ref, r_ref):
    h = h_ref[...]
    ms = jnp.mean(h * h, axis=-1, keepdims=True)
    n = h * lax.rsqrt(ms + EPS) * g_ref[...] * (1.0 + scale_ref[0]) + shift_ref[0]
    n_ref[...] = n
    logits = lax.dot_general(wr_ref[...], n, (((1,), (1,)), ((), ())), preferred_element_type=f32,
                             precision=lax.Precision.HIGHEST)
    score = _sigmoid(logits)
    sel = score + bias_ref[...]
    sel_e = [sel[e:e + 1] for e in range(N_GROUPS * GROUP_SIZE)]
    score_e = [score[e:e + 1] for e in range(N_GROUPS * GROUP_SIZE)]

    def top2_sum(v):
        best = v[0] + v[1]
        for a in range(GROUP_SIZE):
            for b in range(a + 1, GROUP_SIZE):
                if (a, b) != (0, 1):
                    best = jnp.maximum(best, v[a] + v[b])
        return best

    g_best = top2_sum(sel_e[:GROUP_SIZE])
    grp = jnp.zeros(g_best.shape, jnp.int32)
    for g in range(1, N_GROUPS):
        cand = top2_sum(sel_e[g * GROUP_SIZE:(g + 1) * GROUP_SIZE])
        better = cand > g_best
        grp = jnp.where(better, g, grp)
        g_best = jnp.where(better, cand, g_best)

    def in_group(rows):
        out = []
        for j in range(GROUP_SIZE):
            v = rows[j]
            for g in range(1, N_GROUPS):
                v = jnp.where(grp == g, rows[g * GROUP_SIZE + j], v)
            out.append(v)
        return out

    v, sc = in_group(sel_e), in_group(score_e)
    j1, b1, c1 = jnp.zeros(grp.shape, jnp.int32), v[0], sc[0]
    for j in range(1, GROUP_SIZE):
        better = v[j] > b1
        j1, b1, c1 = jnp.where(better, j, j1), jnp.where(better, v[j], b1), jnp.where(better, sc[j], c1)
    j2 = jnp.full(grp.shape, -1, jnp.int32)
    b2, c2 = jnp.full(b1.shape, -jnp.inf, f32), jnp.zeros(b1.shape, f32)
    for j in range(GROUP_SIZE):
        better = (j1 != j) & ((v[j] > b2) | (j2 < 0))
        j2, b2, c2 = jnp.where(better, j, j2), jnp.where(better, v[j], b2), jnp.where(better, sc[j], c2)
    tot = c1 + c2
    w1, w2 = c1 / tot, c2 / tot
    first_low = j1 < j2
    lo, hi = jnp.minimum(j1, j2), jnp.maximum(j1, j2)
    pair_in_group = jnp.where(lo == 0, hi - 1, jnp.where(lo == 1, hi + 1, 5))
    r_ref[...] = jnp.zeros(r_ref.shape, f32)
    r_ref[0:1, :] = (grp * 6 + pair_in_group).astype(f32)
    r_ref[1:2, :] = jnp.where(first_low, w1, w2)
    r_ref[2:3, :] = jnp.where(first_low, w2, w1)


def _norm_route(h, g, mods, layer, seg_of, w_router_t, bias, rows):
    t, d = h.shape
    e = w_router_t.shape[0]
    tr = ROW_TILE

    def mod_spec(k):
        return pl.BlockSpec((1, 1, d), lambda i: ((layer * 8 + seg_of(i, tr)) * 6 + k, 0, 0))

    return pl.pallas_call(
        _route_kernel,
        out_shape=[jax.ShapeDtypeStruct((rows, d), f32), jax.ShapeDtypeStruct((8, rows), f32)],
        grid=(rows // tr,),
        in_specs=[pl.BlockSpec((tr, d), lambda i: (i, 0)), pl.BlockSpec((1, d), lambda i: (0, 0)),
                  mod_spec(3), mod_spec(4), pl.BlockSpec((e, d), lambda i: (0, 0)),
                  pl.BlockSpec((e, 1), lambda i: (0, 0))],
        out_specs=[pl.BlockSpec((tr, d), lambda i: (i, 0)), pl.BlockSpec((8, tr), lambda i: (0, i))],
        compiler_params=_params("arbitrary"),
        name="norm_route",
    )(h, g.reshape(1, d), mods, mods, w_router_t, bias.reshape(e, 1).astype(f32))


def _moe_kernel(lo_ref, hi_ref, nv_ref, tok_ref, x_hbm, w_ref, wg_lo, wu_lo, wd_lo, wg_hi, wu_hi, wd_hi,
                out_hbm, xbuf, ybuf, sem_in, sem_out):
    blk = pl.program_id(0)
    nv = nv_ref[blk]
    base = blk * MOE_BLOCK

    def row_in(r):
        return pltpu.make_async_copy(x_hbm.at[pl.ds(tok_ref[base + r], 1)], xbuf.at[pl.ds(r, 1)], sem_in)

    def row_out(r):
        return pltpu.make_async_copy(ybuf.at[pl.ds(r, 1)], out_hbm.at[pl.ds(tok_ref[base + r], 1)], sem_out)

    @pl.when(nv > 0)
    def _():
        def start_in(r, c):
            row_in(r).start()
            return c

        def wait_in(r, c):
            row_in(r).wait()
            return c

        lax.fori_loop(0, MOE_BLOCK, start_in, 0)
        lax.fori_loop(0, MOE_BLOCK, wait_in, 0)
        x = xbuf[...].astype(bf16)
        eye = (lax.broadcasted_iota(jnp.int32, (MOE_BLOCK, MOE_BLOCK), 0)
               == lax.broadcasted_iota(jnp.int32, (MOE_BLOCK, MOE_BLOCK), 1))
        w = w_ref[0]

        def expert(wg, wu, wd, w_row):
            a = jnp.dot(x, wg[0], preferred_element_type=f32)
            u = jnp.dot(x, wu[0], preferred_element_type=f32)
            y = jnp.dot((a * _sigmoid(a) * u).astype(bf16), wd[0], preferred_element_type=f32)
            return y * jnp.sum(jnp.where(eye, w_row, 0.0), axis=1, keepdims=True)

        ybuf[...] = expert(wg_lo, wu_lo, wd_lo, w[0:1]) + expert(wg_hi, wu_hi, wd_hi, w[1:2])

        def start_out(r, c):
            row_out(r).start()
            return c

        def wait_out(r, c):
            row_out(r).wait()
            return c

        lax.fori_loop(0, nv, start_out, 0)
        lax.fori_loop(0, nv, wait_out, 0)


def _moe(n2, route, w_gate, w_up, w_down, n_tok):
    d = n2.shape[1]
    dff = w_gate.shape[-1]
    blk = MOE_BLOCK
    n_blk = -(-n_tok // blk) + N_PAIRS
    n_pad = n_blk * blk
    pair = route[0, :n_tok].astype(jnp.int32)
    onehot = (pair[:, None] == jnp.arange(N_PAIRS, dtype=jnp.int32)[None, :]).astype(jnp.int32)
    csum = jnp.cumsum(onehot, axis=0)
    rank = jnp.sum(csum * onehot, axis=1) - 1
    counts = csum[-1]
    padded = (counts + blk - 1) // blk * blk
    pad_end = jnp.cumsum(padded)
    pad_start = pad_end - padded
    dest = pad_start[pair] + rank
    buf_tok = jnp.zeros((n_pad,), jnp.int32).at[dest].set(jnp.arange(n_tok, dtype=jnp.int32))
    wts = jnp.zeros((2, n_pad), f32).at[:, dest].set(route[1:3, :n_tok])
    wts = wts.reshape(2, n_blk, blk).transpose(1, 0, 2)
    blk_start = jnp.arange(n_blk, dtype=jnp.int32) * blk
    seg = jnp.minimum(jnp.searchsorted(pad_end, blk_start, side="right"), N_PAIRS - 1).astype(jnp.int32)
    nv = jnp.clip(counts[seg] - (blk_start - pad_start[seg]), 0, blk).astype(jnp.int32)
    q = seg % 6
    e_lo = (seg // 6) * GROUP_SIZE + (q >= 3).astype(jnp.int32) + (q >= 5).astype(jnp.int32)
    e_hi = (seg // 6) * GROUP_SIZE + jnp.where(q < 3, q + 1, jnp.where(q < 5, q - 1, 3))

    def w_in(sel):
        return pl.BlockSpec((1, d, dff), lambda i, lo, hi, nvr, tok: ((lo, hi)[sel][i], 0, 0),
                            pipeline_mode=pl.Buffered(1))

    def w_out(sel):
        return pl.BlockSpec((1, dff, d), lambda i, lo, hi, nvr, tok: ((lo, hi)[sel][i], 0, 0),
                            pipeline_mode=pl.Buffered(1))

    return pl.pallas_call(
        _moe_kernel,
        out_shape=jax.ShapeDtypeStruct((n_tok, d), f32),
        grid_spec=pltpu.PrefetchScalarGridSpec(
            num_scalar_prefetch=4,
            grid=(n_blk,),
            in_specs=[pl.BlockSpec(memory_space=pl.ANY),
                      pl.BlockSpec((1, 2, blk), lambda i, lo, hi, nvr, tok: (i, 0, 0)),
                      w_in(0), w_in(0), w_out(0), w_in(1), w_in(1), w_out(1)],
            out_specs=pl.BlockSpec(memory_space=pl.ANY),
            scratch_shapes=[pltpu.VMEM((blk, d), f32), pltpu.VMEM((blk, d), f32),
                            pltpu.SemaphoreType.DMA, pltpu.SemaphoreType.DMA],
        ),
        compiler_params=_params("arbitrary"),
        name="moe",
    )(e_lo, e_hi, nv, buf_tok, n2, wts, w_gate, w_up, w_down, w_gate, w_up, w_down)


def _rope_tables(n_rows, hd, pad_rows):
    n_freq = hd // 4
    inv_freq = ROPE_THETA ** (-jnp.arange(n_freq, dtype=f32) / n_freq)
    pos = jnp.arange(n_rows * GRID_W)
    ang_row = (pos // GRID_W).astype(f32)[:, None] * inv_freq
    ang_col = (pos % GRID_W).astype(f32)[:, None] * inv_freq
    ang = jnp.concatenate([ang_row, ang_row, ang_col, ang_col], axis=1)
    sign = jnp.tile(jnp.concatenate([-jnp.ones((n_freq,), f32), jnp.ones((n_freq,), f32)]), 2)
    cos_t = jnp.concatenate([jnp.cos(ang), jnp.ones((pad_rows, hd), f32)], axis=0)
    sin_t = jnp.concatenate([jnp.sin(ang) * sign, jnp.zeros((pad_rows, hd), f32)], axis=0)
    return cos_t, sin_t


def kernel(x, c, ctx, c_ctx, w_ada, b_ada, g_norm1, g_norm2, w_in, q_norm_g, k_norm_g, diff_lambda, attn_head_g,
           conv_w, conv_b, mlstm_gate_b, mlstm_head_g, w_br_attn, w_br_mlstm, w_out, w_router, router_bias,
           w_gate, w_up, w_down):
    b, s, d = x.shape
    l = ctx.shape[1]
    depth = w_ada.shape[0]
    hd = q_norm_g.shape[-1]
    a_w = w_br_attn.shape[1]
    heads = a_w // (2 * hd)
    mh, dv = mlstm_head_g.shape[-2:]
    qk_w = conv_w.shape[-1] // 2
    dk = qk_w // mh
    m_w = mh * dv
    ng = 4 * mh
    assert w_router.shape[1] == N_GROUPS * GROUP_SIZE
    assert s % ROW_TILE == 0 and l % ROW_TILE == 0 and s % GRID_W == 0 and b + 1 <= 8
    t_x, t_all = b * s, b * s + b * l
    gate_col = 3 * a_w + 2 * qk_w + 2 * m_w
    mq_col, mv_col, mo_col, ga_col = 3 * a_w, 3 * a_w + 2 * qk_w, 3 * a_w + 2 * qk_w + m_w, gate_col

    def seg_of(i, tile):
        return jnp.where(i < t_x // tile, i // (s // tile), b)

    cv = jnp.zeros((8, d), f32).at[:b].set(c).at[b].set(c_ctx)
    mods = _ada(cv, w_ada, b_ada).reshape(depth * 8 * 6, 1, d)
    cos_t, sin_t = _rope_tables(s // GRID_W, hd, ROW_TILE)
    h = jnp.concatenate([x.reshape(t_x, d), ctx.reshape(b * l, d)], axis=0)
    f = None
    for layer in range(depth):
        need_ctx = layer < depth - 1
        rows = t_all if need_ctx else t_x
        lam_init = 0.8 - 0.6 * math.exp(-0.3 * layer)
        lv = diff_lambda[layer].astype(f32)
        lam = (jnp.exp(jnp.sum(lv[0] * lv[1])) - jnp.exp(jnp.sum(lv[2] * lv[3])) + lam_init).reshape(1, 1)
        w_l = w_in[layer]
        w_main = jnp.concatenate([w_l[:, :gate_col], w_l[:, gate_col + ng:]], axis=1).astype(bf16)
        w_g = w_l[:, gate_col:gate_col + ng].astype(bf16)

        if f is None:
            (n1,) = _resid_norm(h, t_all, seg_of, mods, norm=(g_norm1[layer], layer, 0, 1))
        else:
            h, n1 = _resid_norm(h, t_all, seg_of, mods, resid=(f, layer - 1, 5), norm=(g_norm1[layer], layer, 0, 1))
        parts = _matmul(n1, w_main, bf16)
        g_col, c_col = _gates(n1, w_g, mlstm_gate_b[layer], mh)

        qk_gain = jnp.concatenate([jnp.tile(q_norm_g[layer].astype(f32), 2 * heads) * hd ** -0.5,
                                   jnp.tile(k_norm_g[layer].astype(f32), 2 * heads)]).reshape(1, 2 * a_w)
        qk2 = _qk_prep(parts, qk_gain, cos_t, sin_t, 2 * a_w, t_x // ROW_TILE, s // ROW_TILE, hd)
        out_gain = (attn_head_g[layer].astype(f32) * (1.0 - lam_init)).reshape(1, 2 * hd)
        a_flat = _attention(qk2, parts, lam, out_gain, (b, s, l, heads, hd), need_ctx)

        post = jnp.concatenate([jnp.ones((qk_w,), f32), jnp.full((qk_w,), dk ** -0.5, f32)]).reshape(1, 2 * qk_w)
        mqk = _mlstm_prep(parts, mq_col, conv_w[layer].astype(f32), conv_b[layer].reshape(1, -1).astype(f32), post,
                          (b, s, l))
        hdir = _mlstm_scan(mqk, parts, mv_col, g_col, c_col, g_col.T, c_col.T, (b, s, l, mh, dk, dv))
        m_flat = _mlstm_out(hdir, parts, mo_col, mlstm_head_g[layer], rows)

        y = _merge(a_flat, m_flat, w_br_attn[layer].astype(bf16), w_br_mlstm[layer].astype(bf16), parts, ga_col, rows)
        h = _out_proj(y, w_out[layer].astype(bf16), h, mods, layer, seg_of, rows)
        n2, route = _norm_route(h, g_norm2[layer], mods, layer, seg_of, w_router.T.astype(f32), router_bias, rows)
        f = _moe(n2, route, w_gate[layer].astype(bf16), w_up[layer].astype(bf16), w_down[layer].astype(bf16), rows)
    (h,) = _resid_norm(h, t_x, seg_of, mods, resid=(f, depth - 1, 5))
    return h[:t_x].reshape(b, s, d)
```

```python
import functools
import math

import jax
import jax.numpy as jnp
from jax import lax
from jax.experimental import pallas as pl
from jax.experimental.pallas import tpu as pltpu

GRID_W = 64
EPS = 1e-6
ROPE_THETA = 10000.0
M_CHUNK = 128
M_CONV = 3
GATE_CAP = 15.0
N_GROUPS = 4
GROUP_SIZE = 4
N_PAIRS = N_GROUPS * 6
NEG = -1e30

V7X_VMEM_LIMIT = 56 * 1024 * 1024
ROW_TILE = 256
MOE_BLOCK = 128
BF16_ROWS = 16
ATTN_MAX_CHUNK = 1536

f32 = jnp.float32
bf16 = jnp.bfloat16


def _params(*sem):
    return pltpu.CompilerParams(dimension_semantics=sem, vmem_limit_bytes=V7X_VMEM_LIMIT)


def _sigmoid(x):
    return 1.0 / (1.0 + jnp.exp(-x))


def _ada_kernel(cv_ref, w_ref, b_ref, o_ref):
    cv = cv_ref[...]
    s = (cv * _sigmoid(cv)).astype(bf16)
    o_ref[0] = jnp.dot(s, w_ref[0].astype(bf16), preferred_element_type=f32) + b_ref[0]


def _ada(cv, w_ada, b_ada):
    depth, d, n6 = w_ada.shape
    tn = min(512, n6)
    return pl.pallas_call(
        _ada_kernel,
        out_shape=jax.ShapeDtypeStruct((depth, 8, n6), f32),
        grid=(depth, n6 // tn),
        in_specs=[
            pl.BlockSpec((8, d), lambda l, j: (0, 0)),
            pl.BlockSpec((1, d, tn), lambda l, j: (l, 0, j)),
            pl.BlockSpec((1, 1, tn), lambda l, j: (l, 0, j)),
        ],
        out_specs=pl.BlockSpec((1, 8, tn), lambda l, j: (l, 0, j)),
        compiler_params=_params("arbitrary", "arbitrary"),
        name="ada",
    )(cv, w_ada, b_ada.reshape(depth, 1, n6))


def _norm_kernel(*refs, has_resid, has_norm):
    it = iter(refs)
    h_ref = next(it)
    if has_resid:
        f_ref, gate_ref = next(it), next(it)
    if has_norm:
        g_ref, shift_ref, scale_ref = next(it), next(it), next(it)
    h = h_ref[...]
    if has_resid:
        h = h + gate_ref[0] * f_ref[...]
        next(it)[...] = h
    if has_norm:
        ms = jnp.mean(h * h, axis=-1, keepdims=True)
        y = h * lax.rsqrt(ms + EPS) * g_ref[...]
        n_ref = next(it)
        n_ref[...] = (y * (1.0 + scale_ref[0]) + shift_ref[0]).astype(n_ref.dtype)


def _resid_norm(h, rows, seg_of, mods, resid=None, norm=None, n_dtype=bf16):
    t, d = h.shape
    tr = ROW_TILE
    row_spec = pl.BlockSpec((tr, d), lambda i: (i, 0))

    def mod_spec(layer, k):
        return pl.BlockSpec((1, 1, d), lambda i: ((layer * 8 + seg_of(i, tr)) * 6 + k, 0, 0))

    args, in_specs, out_shape, out_specs, aliases = [h], [row_spec], [], [], {}
    if resid is not None:
        f, layer, k_gate = resid
        args += [f, mods]
        in_specs += [row_spec, mod_spec(layer, k_gate)]
        out_shape.append(jax.ShapeDtypeStruct((t, d), f32))
        out_specs.append(row_spec)
        aliases = {0: 0}
    if norm is not None:
        g, layer, k_shift, k_scale = norm
        args += [g.reshape(1, d), mods, mods]
        in_specs += [pl.BlockSpec((1, d), lambda i: (0, 0)), mod_spec(layer, k_shift), mod_spec(layer, k_scale)]
        out_shape.append(jax.ShapeDtypeStruct((t, d), n_dtype))
        out_specs.append(row_spec)
    outs = pl.pallas_call(
        functools.partial(_norm_kernel, has_resid=resid is not None, has_norm=norm is not None),
        out_shape=out_shape,
        grid=(rows // tr,),
        in_specs=in_specs,
        out_specs=out_specs,
        input_output_aliases=aliases,
        compiler_params=_params("arbitrary"),
        name="resid_norm",
    )(*args)
    return outs


def _mm_kernel(a_ref, b_ref, o_ref):
    o_ref[...] = jnp.dot(a_ref[...], b_ref[...], preferred_element_type=f32).astype(o_ref.dtype)


def _matmul(a, b, out_dtype, tm=512, tn=1024):
    m, k = a.shape
    n = b.shape[1]
    tm, tn = min(tm, m), math.gcd(tn, n)
    assert m % tm == 0 and tn % 128 == 0
    return pl.pallas_call(
        _mm_kernel,
        out_shape=jax.ShapeDtypeStruct((m, n), out_dtype),
        grid=(n // tn, m // tm),
        in_specs=[pl.BlockSpec((tm, k), lambda j, i: (i, 0)), pl.BlockSpec((k, tn), lambda j, i: (0, j))],
        out_specs=pl.BlockSpec((tm, tn), lambda j, i: (i, j)),
        compiler_params=_params("arbitrary", "arbitrary"),
        name="in_proj",
    )(a, b)


def _gates_kernel(n_ref, w_ref, b_ref, g_ref, c_ref, *, mh):
    tr = n_ref.shape[0]
    mg = jnp.dot(n_ref[...], w_ref[...], preferred_element_type=f32) + b_ref[...]
    g = GATE_CAP * jnp.tanh(mg * (1.0 / GATE_CAP))
    lane = lax.broadcasted_iota(jnp.int32, g.shape, 1)
    is_forget = (lane // mh) % 2 == 1
    log_sig = jnp.minimum(g, 0.0) - jnp.log(1.0 + jnp.exp(-jnp.abs(g)))
    val = jnp.where(is_forget, log_sig, g)
    g_ref[...] = val
    r = lax.broadcasted_iota(jnp.int32, (M_CHUNK, M_CHUNK), 0)
    c = lax.broadcasted_iota(jnp.int32, (M_CHUNK, M_CHUNK), 1)
    lower = (c <= r).astype(f32)
    upper = (c >= r).astype(f32)
    fwd = lax.broadcasted_iota(jnp.int32, (M_CHUNK, g.shape[1]), 1) < 2 * mh
    for ch in range(tr // M_CHUNK):
        v = val[ch * M_CHUNK:(ch + 1) * M_CHUNK]
        cf = jnp.dot(lower, v, preferred_element_type=f32, precision=lax.Precision.HIGHEST)
        cb = jnp.dot(upper, v, preferred_element_type=f32, precision=lax.Precision.HIGHEST)
        c_ref[ch * M_CHUNK:(ch + 1) * M_CHUNK, :] = jnp.where(fwd, cf, cb)


def _gates(n, w_g, gate_b, mh):
    t, d = n.shape
    ng = 4 * mh
    tr = ROW_TILE
    spec = pl.BlockSpec((tr, ng), lambda i: (i, 0))
    return pl.pallas_call(
        functools.partial(_gates_kernel, mh=mh),
        out_shape=[jax.ShapeDtypeStruct((t, ng), f32)] * 2,
        grid=(t // tr,),
        in_specs=[pl.BlockSpec((tr, d), lambda i: (i, 0)), pl.BlockSpec((d, ng), lambda i: (0, 0)),
                  pl.BlockSpec((1, ng), lambda i: (0, 0))],
        out_specs=[spec, spec],
        compiler_params=_params("arbitrary"),
        name="gates",
    )(n, w_g, gate_b.reshape(1, ng).astype(f32))


def _qk_kernel(x_ref, g_ref, cos_ref, sin_ref, o_ref, *, hd):
    x = x_ref[...].astype(f32)
    cos, sin = cos_ref[...], sin_ref[...]
    lane = lax.broadcasted_iota(jnp.int32, cos.shape, 1)
    first_half = (lane % (hd // 2)) < hd // 4
    for s in range(x.shape[1] // hd):
        sl = slice(s * hd, (s + 1) * hd)
        xs = x[:, sl]
        ms = jnp.mean(xs * xs, axis=-1, keepdims=True)
        y = xs * lax.rsqrt(ms + EPS) * g_ref[:, sl]
        partner = jnp.where(first_half, pltpu.roll(y, hd - hd // 4, 1), pltpu.roll(y, hd // 4, 1))
        o_ref[:, sl] = (y * cos + partner * sin).astype(o_ref.dtype)


def _qk_prep(parts, gain, cos_t, sin_t, width, n_x_tiles, tiles_per_seq, hd):
    t = parts.shape[0]
    tr = ROW_TILE
    wq = min(512, width)

    def tbl(i, j):
        return (jnp.where(i < n_x_tiles, i % tiles_per_seq, tiles_per_seq), 0)

    return pl.pallas_call(
        functools.partial(_qk_kernel, hd=hd),
        out_shape=jax.ShapeDtypeStruct((t, width), bf16),
        grid=(t // tr, width // wq),
        in_specs=[pl.BlockSpec((tr, wq), lambda i, j: (i, j)), pl.BlockSpec((1, wq), lambda i, j: (0, j)),
                  pl.BlockSpec((tr, hd), tbl), pl.BlockSpec((tr, hd), tbl)],
        out_specs=pl.BlockSpec((tr, wq), lambda i, j: (i, j)),
        compiler_params=_params("arbitrary", "arbitrary"),
        name="qk_prep",
    )(parts, gain, cos_t, sin_t)


def _attn_kernel(q_ref, kx_ref, kc_ref, vx_ref, vc_ref, lam_ref, g_ref, o_ref, k_ref, vt_ref, *, hd, tkv, tt, nq_x,
                 with_ctx_queries):
    i = pl.program_id(2)
    tq = q_ref.shape[0]
    l_ctx, s = kc_ref.shape[0], kx_ref.shape[0]

    @pl.when(i == 0)
    def _():
        k_ref[0:l_ctx] = kc_ref[...]
        vt_ref[:, 0:l_ctx] = vc_ref[...].T
        for c in range(s // tt):
            k_ref[l_ctx + c * tt:l_ctx + (c + 1) * tt] = kx_ref[c * tt:(c + 1) * tt]
            vt_ref[:, l_ctx + c * tt:l_ctx + (c + 1) * tt] = vx_ref[c * tt:(c + 1) * tt].T

    def attend(n_keys, tk):
        qt = q_ref[...].T
        qt0, qt1 = qt[:hd], qt[hd:]

        def scores(c):
            k = k_ref[c * tk:(c + 1) * tk]
            return jnp.concatenate([jnp.dot(k[:, :hd], qt0, preferred_element_type=f32),
                                    jnp.dot(k[:, hd:], qt1, preferred_element_type=f32)], axis=1)

        m = l_sum = acc = None
        st = scores(0)
        for c in range(n_keys // tk):
            st_next = scores(c + 1) if (c + 1) * tk < n_keys else None
            m_c = jnp.max(st, axis=0, keepdims=True)
            m_new = m_c if m is None else jnp.maximum(m, m_c)
            p = jnp.exp(st - m_new)
            pv = jnp.dot(vt_ref[:, c * tk:(c + 1) * tk], p.astype(bf16), preferred_element_type=f32)
            if m is None:
                l_sum, acc = jnp.sum(p, axis=0, keepdims=True), pv
            else:
                alpha = jnp.exp(m - m_new)
                l_sum, acc = alpha * l_sum + jnp.sum(p, axis=0, keepdims=True), alpha * acc + pv
            m, st = m_new, st_next
        acc = acc / l_sum
        a = (acc[:, :tq] - lam_ref[...] * acc[:, tq:]).T
        ms = jnp.mean(a * a, axis=-1, keepdims=True)
        o_ref[...] = (a * lax.rsqrt(ms + EPS) * g_ref[...]).astype(o_ref.dtype)

    if with_ctx_queries:
        pl.when(i < nq_x)(lambda: attend(l_ctx + s, tkv))
        pl.when(i >= nq_x)(lambda: attend(l_ctx, l_ctx))
    else:
        attend(l_ctx + s, tkv)


def _attention(qk2, parts, lam, out_gain, dims, with_ctx_queries):
    b, s, l, heads, hd = dims
    a_w = heads * 2 * hd
    t = qk2.shape[0]
    tq = ROW_TILE
    n_keys = s + l
    tkv = next(n_keys // n for n in range(1, n_keys) if n_keys % (n * 128) == 0 and n_keys // n <= ATTN_MAX_CHUNK)
    nq_x, nq_c = s // tq, l // tq
    nq = nq_x + (nq_c if with_ctx_queries else 0)
    hw = 2 * hd
    k_col, v_col = a_w // hw, 2 * a_w // hw
    ctx_row = b * s // l

    def q_idx(bi, h, i):
        return (jnp.where(i < nq_x, bi * nq_x + i, b * nq_x + bi * nq_c + (i - nq_x)), h)

    return pl.pallas_call(
        functools.partial(_attn_kernel, hd=hd, tkv=tkv, tt=min(512, s), nq_x=nq_x,
                          with_ctx_queries=with_ctx_queries),
        out_shape=jax.ShapeDtypeStruct((t if with_ctx_queries else b * s, a_w), bf16),
        grid=(b, heads, nq),
        in_specs=[
            pl.BlockSpec((tq, hw), q_idx),
            pl.BlockSpec((s, hw), lambda bi, h, i: (bi, k_col + h)),
            pl.BlockSpec((l, hw), lambda bi, h, i: (ctx_row + bi, k_col + h)),
            pl.BlockSpec((s, hw), lambda bi, h, i: (bi, v_col + h)),
            pl.BlockSpec((l, hw), lambda bi, h, i: (ctx_row + bi, v_col + h)),
            pl.BlockSpec((1, 1), lambda bi, h, i: (0, 0)),
            pl.BlockSpec((1, hw), lambda bi, h, i: (0, 0)),
        ],
        out_specs=pl.BlockSpec((tq, hw), q_idx),
        scratch_shapes=[pltpu.VMEM((n_keys, hw), bf16), pltpu.VMEM((hw, n_keys), bf16)],
        compiler_params=_params("arbitrary", "arbitrary", "arbitrary"),
        name="diff_attn",
    )(qk2, qk2, qk2, parts, parts, lam, out_gain)


def _mprep_kernel(x_ref, p_ref, n_ref, w_ref, b_ref, sc_ref, o_ref, *, s, l, bs):
    tr = x_ref.shape[0]
    row0 = pl.program_id(0) * tr
    in_x = row0 < bs
    rel = jnp.where(in_x, row0 % s, (row0 - bs) % l)
    seq_len = jnp.where(in_x, s, l)
    keep_prev = jnp.where(rel == 0, 0.0, 1.0)
    keep_next = jnp.where(rel + tr == seq_len, 0.0, 1.0)
    x = x_ref[...].astype(f32)
    prev = p_ref[...].astype(f32)[BF16_ROWS - 1:BF16_ROWS] * keep_prev
    nxt = n_ref[...].astype(f32)[0:1] * keep_next
    rows = lax.broadcasted_iota(jnp.int32, x.shape, 0)
    xm = jnp.where(rows == 0, prev, pltpu.roll(x, 1, 0))
    xp = jnp.where(rows == tr - 1, nxt, pltpu.roll(x, tr - 1, 0))
    w = w_ref[...]
    u = b_ref[...] + xm * w[0:1] + x * w[1:2] + xp * w[2:3]
    o_ref[...] = (u * _sigmoid(u) * sc_ref[...]).astype(o_ref.dtype)


def _mlstm_prep(parts, col0, conv_w, conv_b, post, dims):
    b, s, l = dims
    t = parts.shape[0]
    width = conv_w.shape[1]
    tr = ROW_TILE
    wc = min(512, width)
    cb = col0 // wc
    sub = tr // BF16_ROWS
    last = t // BF16_ROWS - 1
    return pl.pallas_call(
        functools.partial(_mprep_kernel, s=s, l=l, bs=b * s),
        out_shape=jax.ShapeDtypeStruct((t, width), bf16),
        grid=(t // tr, width // wc),
        in_specs=[
            pl.BlockSpec((tr, wc), lambda i, j: (i, cb + j)),
            pl.BlockSpec((BF16_ROWS, wc), lambda i, j: (jnp.maximum(i * sub - 1, 0), cb + j)),
            pl.BlockSpec((BF16_ROWS, wc), lambda i, j: (jnp.minimum((i + 1) * sub, last), cb + j)),
            pl.BlockSpec((M_CONV, wc), lambda i, j: (0, j)),
            pl.BlockSpec((1, wc), lambda i, j: (0, j)),
            pl.BlockSpec((1, wc), lambda i, j: (0, j)),
        ],
        out_specs=pl.BlockSpec((tr, wc), lambda i, j: (i, j)),
        compiler_params=_params("arbitrary", "arbitrary"),
        name="mlstm_prep",
    )(parts, parts, parts, conv_w, conv_b, post)


def _scan_chunk(d, h, mh, q, k, v, gcol, grow_ref, c_st, n_st, m_st):
    ng = 4 * mh
    col_i = d * 2 * mh + h
    col_f = ng + col_i + mh
    lane = lax.broadcasted_iota(jnp.int32, gcol.shape, 1)
    li_col = jnp.sum(jnp.where(lane == col_i, gcol, 0.0), axis=1, keepdims=True)
    bs_col = jnp.sum(jnp.where(lane == col_f, gcol, 0.0), axis=1, keepdims=True)
    li_row = grow_ref[pl.ds(col_i, 1), :]
    bs_row = grow_ref[pl.ds(col_f, 1), :]
    r = lax.broadcasted_iota(jnp.int32, (M_CHUNK, M_CHUNK), 0)
    c = lax.broadcasted_iota(jnp.int32, (M_CHUNK, M_CHUNK), 1)
    visible = (c <= r) if d == 0 else (c >= r)
    last = M_CHUNK - 1 if d == 0 else 0
    b_end = bs_row[:, last:last + 1]

    d_log = jnp.where(visible, bs_col + (li_row - bs_row), NEG)
    inter = bs_col + m_st
    m_pos = jnp.maximum(jnp.max(d_log, axis=1, keepdims=True), inter)
    w_intra = jnp.exp(d_log - m_pos)
    w_inter = jnp.exp(inter - m_pos)
    sc = lax.dot_general(q, k, (((1,), (1,)), ((), ())), preferred_element_type=f32) * w_intra
    num = jnp.dot(sc.astype(bf16), v, preferred_element_type=f32) + w_inter * jnp.dot(
        q, c_st.astype(bf16), preferred_element_type=f32)
    qn = jnp.sum(q.astype(f32) * n_st, axis=1, keepdims=True)
    den = jnp.sum(sc, axis=1, keepdims=True) + w_inter * qn
    hidden = num / jnp.maximum(jnp.abs(den), jnp.exp(-m_pos))

    w_log = b_end - bs_col + li_col
    m_new = jnp.maximum(b_end + m_st, jnp.max(w_log, axis=0, keepdims=True))
    decay = jnp.exp(b_end + m_st - m_new)
    wk = jnp.exp(w_log - m_new) * k.astype(f32)
    c_new = decay * c_st + lax.dot_general(wk.astype(bf16), v, (((0,), (0,)), ((), ())),
                                           preferred_element_type=f32)
    n_new = decay * n_st + jnp.sum(wk, axis=0, keepdims=True)
    return hidden, c_new, n_new, m_new


def _scan_kernel(*refs, mh):
    h = pl.program_id(1)
    ins, outs, (c_ref, n_ref, m_ref) = refs[:10], refs[10:12], refs[12:]

    @pl.when(pl.program_id(2) == 0)
    def _():
        for ref in (c_ref, n_ref, m_ref):
            ref[...] = jnp.zeros(ref.shape, f32)

    res = []
    for d in range(2):
        q_ref, k_ref, v_ref, gcol_ref, grow_ref = ins[5 * d:5 * d + 5]
        res.append(_scan_chunk(d, h, mh, q_ref[...], k_ref[...], v_ref[...], gcol_ref[...], grow_ref,
                               c_ref[d], n_ref[d], m_ref[d]))
    for d in range(2):
        outs[d][...] = res[d][0].astype(outs[d].dtype)
    c_ref[...] = jnp.stack([res[0][1], res[1][1]])
    n_ref[...] = jnp.stack([res[0][2], res[1][2]])
    m_ref[...] = jnp.stack([res[0][3], res[1][3]])


def _mlstm_scan(mqk, parts, v_col0, gc_col, gc_row, dims):
    b, s, l, mh, dk, dv = dims
    t = mqk.shape[0]
    n_lc, n_sc = l // M_CHUNK, s // M_CHUNK
    ctx0 = b * s // M_CHUNK
    vb = v_col0 // dv
    ng2 = gc_col.shape[1]

    def chunk(d, bi, st):
        in_ctx = st < n_lc
        local = jnp.where(in_ctx, st, st - n_lc)
        n_local = jnp.where(in_ctx, n_lc, n_sc)
        pos = local if d == 0 else n_local - 1 - local
        return jnp.where(in_ctx, ctx0 + bi * n_lc, bi * n_sc) + pos

    def dir_specs(d):
        return [
            pl.BlockSpec((M_CHUNK, dk), lambda bi, h, st: (chunk(d, bi, st), h)),
            pl.BlockSpec((M_CHUNK, dk), lambda bi, h, st: (chunk(d, bi, st), mh + h)),
            pl.BlockSpec((M_CHUNK, dv), lambda bi, h, st: (chunk(d, bi, st), vb + h)),
            pl.BlockSpec((M_CHUNK, ng2), lambda bi, h, st: (chunk(d, bi, st), 0)),
            pl.BlockSpec((ng2, M_CHUNK), lambda bi, h, st: (0, chunk(d, bi, st))),
        ]

    def out_spec(d):
        return pl.BlockSpec((M_CHUNK, dv), lambda bi, h, st: (chunk(d, bi, st), h))

    state = [pltpu.VMEM((2, dk, dv), f32), pltpu.VMEM((2, 1, dk), f32), pltpu.VMEM((2, 1, 1), f32)]
    args = (mqk, mqk, parts, gc_col, gc_row)
    return pl.pallas_call(
        functools.partial(_scan_kernel, mh=mh),
        out_shape=[jax.ShapeDtypeStruct((t, mh * dv), bf16)] * 2,
        grid=(b, mh, n_lc + n_sc),
        in_specs=dir_specs(0) + dir_specs(1),
        out_specs=[out_spec(0), out_spec(1)],
        scratch_shapes=state,
        compiler_params=_params("arbitrary", "arbitrary", "arbitrary"),
        name="mlstm_scan",
    )(*args, *args)


def _mout_kernel(hf_ref, hb_ref, o_ref, g_ref, out_ref):
    hs = hf_ref[...].astype(f32) + hb_ref[...].astype(f32)
    ms = jnp.mean(hs * hs, axis=-1, keepdims=True)
    y = hs * lax.rsqrt(ms + EPS) * g_ref[0]
    out_ref[...] = (y * _sigmoid(o_ref[...].astype(f32))).astype(out_ref.dtype)


def _mlstm_out(h_fwd, h_bwd, parts, o_col0, head_g, rows):
    m_w = h_fwd.shape[1]
    mh, dv = head_g.shape
    tr = ROW_TILE
    ob = o_col0 // dv
    head_spec = pl.BlockSpec((tr, dv), lambda i, h: (i, h))
    return pl.pallas_call(
        _mout_kernel,
        out_shape=jax.ShapeDtypeStruct((rows, m_w), bf16),
        grid=(rows // tr, mh),
        in_specs=[head_spec, head_spec, pl.BlockSpec((tr, dv), lambda i, h: (i, ob + h)),
                  pl.BlockSpec((1, 1, dv), lambda i, h: (h, 0, 0))],
        out_specs=head_spec,
        compiler_params=_params("arbitrary", "arbitrary"),
        name="mlstm_out",
    )(h_fwd, h_bwd, parts, head_g.reshape(mh, 1, dv).astype(f32))


def _merge_kernel(a_ref, wa_ref, m_ref, wm_ref, ga_ref, gb_ref, o_ref):
    ya = jnp.dot(a_ref[...], wa_ref[...], preferred_element_type=f32)
    ym = jnp.dot(m_ref[...], wm_ref[...], preferred_element_type=f32)
    y = _sigmoid(ga_ref[...].astype(f32)) * ya + _sigmoid(gb_ref[...].astype(f32)) * ym
    o_ref[...] = y.astype(o_ref.dtype)


def _merge(a_flat, m_flat, w_a, w_m, parts, ga_col0, rows):
    d = w_a.shape[1]
    tm, tn = 512, min(512, d)
    ga, gb = ga_col0 // tn, (ga_col0 + d) // tn
    return pl.pallas_call(
        _merge_kernel,
        out_shape=jax.ShapeDtypeStruct((rows, d), bf16),
        grid=(d // tn, rows // tm),
        in_specs=[
            pl.BlockSpec((tm, a_flat.shape[1]), lambda j, i: (i, 0)),
            pl.BlockSpec((w_a.shape[0], tn), lambda j, i: (0, j)),
            pl.BlockSpec((tm, m_flat.shape[1]), lambda j, i: (i, 0)),
            pl.BlockSpec((w_m.shape[0], tn), lambda j, i: (0, j)),
            pl.BlockSpec((tm, tn), lambda j, i: (i, ga + j)),
            pl.BlockSpec((tm, tn), lambda j, i: (i, gb + j)),
        ],
        out_specs=pl.BlockSpec((tm, tn), lambda j, i: (i, j)),
        compiler_params=_params("arbitrary", "arbitrary"),
        name="branch_merge",
    )(a_flat, w_a, m_flat, w_m, parts, parts)


def _outproj_kernel(y_ref, w_ref, h_ref, gate_ref, o_ref):
    o_ref[...] = h_ref[...] + gate_ref[0] * jnp.dot(y_ref[...], w_ref[...], preferred_element_type=f32)


def _out_proj(y, w_out, h, mods, layer, seg_of, rows):
    t, d = h.shape
    tm, tn = 512, min(512, d)
    return pl.pallas_call(
        _outproj_kernel,
        out_shape=jax.ShapeDtypeStruct((t, d), f32),
        grid=(d // tn, rows // tm),
        in_specs=[
            pl.BlockSpec((tm, d), lambda j, i: (i, 0)),
            pl.BlockSpec((d, tn), lambda j, i: (0, j)),
            pl.BlockSpec((tm, tn), lambda j, i: (i, j)),
            pl.BlockSpec((1, 1, tn), lambda j, i: ((layer * 8 + seg_of(i, tm)) * 6 + 2, 0, j)),
        ],
        out_specs=pl.BlockSpec((tm, tn), lambda j, i: (i, j)),
        input_output_aliases={2: 0},
        compiler_params=_params("arbitrary", "arbitrary"),
        name="out_proj",
    )(y, w_out, h, mods)


def _route_kernel(h_ref, g_ref, shift_ref, scale_ref, wr_ref, bias_ref, n_ref, r_ref):
    h = h_ref[...]
    ms = jnp.mean(h * h, axis=-1, keepdims=True)
    n = h * lax.rsqrt(ms + EPS) * g_ref[...] * (1.0 + scale_ref[0]) + shift_ref[0]
    n_ref[...] = n
    logits = lax.dot_general(wr_ref[...], n, (((1,), (1,)), ((), ())), preferred_element_type=f32,
                             precision=lax.Precision.HIGHEST)
    score = _sigmoid(logits)
    sel = score + bias_ref[...]
    sel_e = [sel[e:e + 1] for e in range(N_GROUPS * GROUP_SIZE)]
    score_e = [score[e:e + 1] for e in range(N_GROUPS * GROUP_SIZE)]

    def top2_sum(v):
        best = v[0] + v[1]
        for a in range(GROUP_SIZE):
            for b in range(a + 1, GROUP_SIZE):
                if (a, b) != (0, 1):
                    best = jnp.maximum(best, v[a] + v[b])
        return best

    g_best = top2_sum(sel_e[:GROUP_SIZE])
    grp = jnp.zeros(g_best.shape, jnp.int32)
    for g in range(1, N_GROUPS):
        cand = top2_sum(sel_e[g * GROUP_SIZE:(g + 1) * GROUP_SIZE])
        better = cand > g_best
        grp = jnp.where(better, g, grp)
        g_best = jnp.where(better, cand, g_best)

    def in_group(rows):
        out = []
        for j in range(GROUP_SIZE):
            v = rows[j]
            for g in range(1, N_GROUPS):
                v = jnp.where(grp == g, rows[g * GROUP_SIZE + j], v)
            out.append(v)
        return out

    v, sc = in_group(sel_e), in_group(score_e)
    j1, b1, c1 = jnp.zeros(grp.shape, jnp.int32), v[0], sc[0]
    for j in range(1, GROUP_SIZE):
        better = v[j] > b1
        j1, b1, c1 = jnp.where(better, j, j1), jnp.where(better, v[j], b1), jnp.where(better, sc[j], c1)
    j2 = jnp.full(grp.shape, -1, jnp.int32)
    b2, c2 = jnp.full(b1.shape, -jnp.inf, f32), jnp.zeros(b1.shape, f32)
    for j in range(GROUP_SIZE):
        better = (j1 != j) & ((v[j] > b2) | (j2 < 0))
        j2, b2, c2 = jnp.where(better, j, j2), jnp.where(better, v[j], b2), jnp.where(better, sc[j], c2)
    tot = c1 + c2
    w1, w2 = c1 / tot, c2 / tot
    first_low = j1 < j2
    lo, hi = jnp.minimum(j1, j2), jnp.maximum(j1, j2)
    pair_in_group = jnp.where(lo == 0, hi - 1, jnp.where(lo == 1, hi + 1, 5))
    r_ref[...] = jnp.zeros(r_ref.shape, f32)
    r_ref[0:1, :] = (grp * 6 + pair_in_group).astype(f32)
    r_ref[1:2, :] = jnp.where(first_low, w1, w2)
    r_ref[2:3, :] = jnp.where(first_low, w2, w1)


def _norm_route(h, g, mods, layer, seg_of, w_router_t, bias, rows):
    t, d = h.shape
    e = w_router_t.shape[0]
    tr = ROW_TILE

    def mod_spec(k):
        return pl.BlockSpec((1, 1, d), lambda i: ((layer * 8 + seg_of(i, tr)) * 6 + k, 0, 0))

    return pl.pallas_call(
        _route_kernel,
        out_shape=[jax.ShapeDtypeStruct((rows, d), f32), jax.ShapeDtypeStruct((8, rows), f32)],
        grid=(rows // tr,),
        in_specs=[pl.BlockSpec((tr, d), lambda i: (i, 0)), pl.BlockSpec((1, d), lambda i: (0, 0)),
                  mod_spec(3), mod_spec(4), pl.BlockSpec((e, d), lambda i: (0, 0)),
                  pl.BlockSpec((e, 1), lambda i: (0, 0))],
        out_specs=[pl.BlockSpec((tr, d), lambda i: (i, 0)), pl.BlockSpec((8, tr), lambda i: (0, i))],
        compiler_params=_params("arbitrary"),
        name="norm_route",
    )(h, g.reshape(1, d), mods, mods, w_router_t, bias.reshape(e, 1).astype(f32))


def _moe_kernel(lo_ref, hi_ref, nv_ref, tok_ref, x_hbm, w_ref, wg_lo, wu_lo, wd_lo, wg_hi, wu_hi, wd_hi,
                out_hbm, xbuf, ybuf, sem_in, sem_out):
    blk = pl.program_id(0)
    slot = blk % 2

    def row_in(b, r):
        return pltpu.make_async_copy(x_hbm.at[pl.ds(tok_ref[b * MOE_BLOCK + r], 1)],
                                     xbuf.at[b % 2, pl.ds(r, 1)], sem_in.at[b % 2])

    def row_out(b, r):
        return pltpu.make_async_copy(ybuf.at[b % 2, pl.ds(r, 1)],
                                     out_hbm.at[pl.ds(tok_ref[b * MOE_BLOCK + r], 1)], sem_out.at[b % 2])

    def for_rows(n, fn):
        def body(r, c):
            fn(r)
            return c

        lax.fori_loop(0, n, body, 0)

    def has_rows(b):
        n_blk = pl.num_programs(0)
        return jnp.logical_and(b < n_blk, nv_ref[jnp.minimum(b, n_blk - 1)] > 0)

    @pl.when(nv_ref[blk] > 0)
    def _():
        @pl.when(blk == 0)
        def _():
            for_rows(MOE_BLOCK, lambda r: row_in(blk, r).start())

        for_rows(MOE_BLOCK, lambda r: row_in(blk, r).wait())

        @pl.when(has_rows(blk + 1))
        def _():
            for_rows(MOE_BLOCK, lambda r: row_in(blk + 1, r).start())

        x = xbuf[slot].astype(bf16)
        eye = (lax.broadcasted_iota(jnp.int32, (MOE_BLOCK, MOE_BLOCK), 0)
               == lax.broadcasted_iota(jnp.int32, (MOE_BLOCK, MOE_BLOCK), 1))
        w = w_ref[0]

        def expert(wg, wu, wd, w_row):
            a = jnp.dot(x, wg[0], preferred_element_type=f32)
            u = jnp.dot(x, wu[0], preferred_element_type=f32)
            y = jnp.dot((a * _sigmoid(a) * u).astype(bf16), wd[0], preferred_element_type=f32)
            return y * jnp.sum(jnp.where(eye, w_row, 0.0), axis=1, keepdims=True)

        ybuf[slot] = expert(wg_lo, wu_lo, wd_lo, w[0:1]) + expert(wg_hi, wu_hi, wd_hi, w[1:2])

        @pl.when(blk > 0)
        def _():
            for_rows(nv_ref[blk - 1], lambda r: row_out(blk - 1, r).wait())

        for_rows(nv_ref[blk], lambda r: row_out(blk, r).start())

        @pl.when(jnp.logical_not(has_rows(blk + 1)))
        def _():
            for_rows(nv_ref[blk], lambda r: row_out(blk, r).wait())


def _moe(n2, route, w_gate, w_up, w_down, n_tok):
    d = n2.shape[1]
    dff = w_gate.shape[-1]
    blk = MOE_BLOCK
    n_blk = -(-n_tok // blk) + N_PAIRS
    n_pad = n_blk * blk
    pair = route[0, :n_tok].astype(jnp.int32)
    onehot = (pair[:, None] == jnp.arange(N_PAIRS, dtype=jnp.int32)[None, :]).astype(jnp.int32)
    csum = jnp.cumsum(onehot, axis=0)
    rank = jnp.sum(csum * onehot, axis=1) - 1
    counts = csum[-1]
    padded = (counts + blk - 1) // blk * blk
    pad_end = jnp.cumsum(padded)
    pad_start = pad_end - padded
    dest = pad_start[pair] + rank
    buf_tok = jnp.zeros((n_pad,), jnp.int32).at[dest].set(jnp.arange(n_tok, dtype=jnp.int32))
    wts = jnp.zeros((2, n_pad), f32).at[:, dest].set(route[1:3, :n_tok])
    wts = wts.reshape(2, n_blk, blk).transpose(1, 0, 2)
    blk_start = jnp.arange(n_blk, dtype=jnp.int32) * blk
    seg = jnp.minimum(jnp.searchsorted(pad_end, blk_start, side="right"), N_PAIRS - 1).astype(jnp.int32)
    nv = jnp.clip(counts[seg] - (blk_start - pad_start[seg]), 0, blk).astype(jnp.int32)
    q = seg % 6
    e_lo = (seg // 6) * GROUP_SIZE + (q >= 3).astype(jnp.int32) + (q >= 5).astype(jnp.int32)
    e_hi = (seg // 6) * GROUP_SIZE + jnp.where(q < 3, q + 1, jnp.where(q < 5, q - 1, 3))

    def w_in(sel):
        return pl.BlockSpec((1, d, dff), lambda i, lo, hi, nvr, tok: ((lo, hi)[sel][i], 0, 0),
                            pipeline_mode=pl.Buffered(1))

    def w_out(sel):
        return pl.BlockSpec((1, dff, d), lambda i, lo, hi, nvr, tok: ((lo, hi)[sel][i], 0, 0),
                            pipeline_mode=pl.Buffered(1))

    return pl.pallas_call(
        _moe_kernel,
        out_shape=jax.ShapeDtypeStruct((n_tok, d), f32),
        grid_spec=pltpu.PrefetchScalarGridSpec(
            num_scalar_prefetch=4,
            grid=(n_blk,),
            in_specs=[pl.BlockSpec(memory_space=pl.ANY),
                      pl.BlockSpec((1, 2, blk), lambda i, lo, hi, nvr, tok: (i, 0, 0)),
                      w_in(0), w_in(0), w_out(0), w_in(1), w_in(1), w_out(1)],
            out_specs=pl.BlockSpec(memory_space=pl.ANY),
            scratch_shapes=[pltpu.VMEM((2, blk, d), f32), pltpu.VMEM((2, blk, d), f32),
                            pltpu.SemaphoreType.DMA((2,)), pltpu.SemaphoreType.DMA((2,))],
        ),
        compiler_params=_params("arbitrary"),
        name="moe",
    )(e_lo, e_hi, nv, buf_tok, n2, wts, w_gate, w_up, w_down, w_gate, w_up, w_down)


def _rope_tables(n_rows, hd, pad_rows):
    n_freq = hd // 4
    inv_freq = ROPE_THETA ** (-jnp.arange(n_freq, dtype=f32) / n_freq)
    pos = jnp.arange(n_rows * GRID_W)
    ang_row = (pos // GRID_W).astype(f32)[:, None] * inv_freq
    ang_col = (pos % GRID_W).astype(f32)[:, None] * inv_freq
    ang = jnp.concatenate([ang_row, ang_row, ang_col, ang_col], axis=1)
    sign = jnp.tile(jnp.concatenate([-jnp.ones((n_freq,), f32), jnp.ones((n_freq,), f32)]), 2)
    cos_t = jnp.concatenate([jnp.cos(ang), jnp.ones((pad_rows, hd), f32)], axis=0)
    sin_t = jnp.concatenate([jnp.sin(ang) * sign, jnp.zeros((pad_rows, hd), f32)], axis=0)
    return cos_t, sin_t


def kernel(x, c, ctx, c_ctx, w_ada, b_ada, g_norm1, g_norm2, w_in, q_norm_g, k_norm_g, diff_lambda, attn_head_g,
           conv_w, conv_b, mlstm_gate_b, mlstm_head_g, w_br_attn, w_br_mlstm, w_out, w_router, router_bias,
           w_gate, w_up, w_down):
    b, s, d = x.shape
    l = ctx.shape[1]
    depth = w_ada.shape[0]
    hd = q_norm_g.shape[-1]
    a_w = w_br_attn.shape[1]
    heads = a_w // (2 * hd)
    mh, dv = mlstm_head_g.shape[-2:]
    qk_w = conv_w.shape[-1] // 2
    dk = qk_w // mh
    m_w = mh * dv
    ng = 4 * mh
    assert w_router.shape[1] == N_GROUPS * GROUP_SIZE
    assert s % ROW_TILE == 0 and l % ROW_TILE == 0 and s % GRID_W == 0 and b + 1 <= 8
    t_x, t_all = b * s, b * s + b * l
    gate_col = 3 * a_w + 2 * qk_w + 2 * m_w
    mq_col, mv_col, mo_col, ga_col = 3 * a_w, 3 * a_w + 2 * qk_w, 3 * a_w + 2 * qk_w + m_w, gate_col

    def seg_of(i, tile):
        return jnp.where(i < t_x // tile, i // (s // tile), b)

    cv = jnp.zeros((8, d), f32).at[:b].set(c).at[b].set(c_ctx)
    mods = _ada(cv, w_ada, b_ada).reshape(depth * 8 * 6, 1, d)
    cos_t, sin_t = _rope_tables(s // GRID_W, hd, ROW_TILE)
    h = jnp.concatenate([x.reshape(t_x, d), ctx.reshape(b * l, d)], axis=0)
    f = None
    for layer in range(depth):
        need_ctx = layer < depth - 1
        rows = t_all if need_ctx else t_x
        lam_init = 0.8 - 0.6 * math.exp(-0.3 * layer)
        lv = diff_lambda[layer].astype(f32)
        lam = (jnp.exp(jnp.sum(lv[0] * lv[1])) - jnp.exp(jnp.sum(lv[2] * lv[3])) + lam_init).reshape(1, 1)
        w_l = w_in[layer]
        w_main = jnp.concatenate([w_l[:, :gate_col], w_l[:, gate_col + ng:]], axis=1).astype(bf16)
        w_g = w_l[:, gate_col:gate_col + ng].astype(bf16)

        if f is None:
            (n1,) = _resid_norm(h, t_all, seg_of, mods, norm=(g_norm1[layer], layer, 0, 1))
        else:
            h, n1 = _resid_norm(h, t_all, seg_of, mods, resid=(f, layer - 1, 5), norm=(g_norm1[layer], layer, 0, 1))
        parts = _matmul(n1, w_main, bf16)
        g_col, c_col = _gates(n1, w_g, mlstm_gate_b[layer], mh)

        qk_gain = jnp.concatenate([jnp.tile(q_norm_g[layer].astype(f32), 2 * heads) * hd ** -0.5,
                                   jnp.tile(k_norm_g[layer].astype(f32), 2 * heads)]).reshape(1, 2 * a_w)
        qk2 = _qk_prep(parts, qk_gain, cos_t, sin_t, 2 * a_w, t_x // ROW_TILE, s // ROW_TILE, hd)
        out_gain = (attn_head_g[layer].astype(f32) * (1.0 - lam_init)).reshape(1, 2 * hd)
        a_flat = _attention(qk2, parts, lam, out_gain, (b, s, l, heads, hd), need_ctx)

        post = jnp.concatenate([jnp.ones((qk_w,), f32), jnp.full((qk_w,), dk ** -0.5, f32)]).reshape(1, 2 * qk_w)
        mqk = _mlstm_prep(parts, mq_col, conv_w[layer].astype(f32), conv_b[layer].reshape(1, -1).astype(f32), post,
                          (b, s, l))
        gc_col = jnp.concatenate([g_col, c_col], axis=1)
        h_fwd, h_bwd = _mlstm_scan(mqk, parts, mv_col, gc_col, gc_col.T, (b, s, l, mh, dk, dv))
        m_flat = _mlstm_out(h_fwd, h_bwd, parts, mo_col, mlstm_head_g[layer], rows)

        y = _merge(a_flat, m_flat, w_br_attn[layer].astype(bf16), w_br_mlstm[layer].astype(bf16), parts, ga_col, rows)
        h = _out_proj(y, w_out[layer].astype(bf16), h, mods, layer, seg_of, rows)
        n2, route = _norm_route(h, g_norm2[layer], mods, layer, seg_of, w_router.T.astype(f32), router_bias, rows)
        f = _moe(n2, route, w_gate[layer].astype(bf16), w_up[layer].astype(bf16), w_down[layer].astype(bf16), rows)
    (h,) = _resid_norm(h, t_x, seg_of, mods, resid=(f, depth - 1, 5))
    return h[:t_x].reshape(b, s, d)
```

```python
import functools
import math

import jax
import jax.numpy as jnp
from jax import lax
from jax.experimental import pallas as pl
from jax.experimental.pallas import tpu as pltpu

GRID_W = 64
EPS = 1e-6
ROPE_THETA = 10000.0
M_CHUNK = 128
M_CONV = 3
GATE_CAP = 15.0
N_GROUPS = 4
GROUP_SIZE = 4
PAIR_ORDER = ((0, 1), (0, 2), (1, 2), (1, 3), (0, 3), (2, 3))
N_PAIRS = N_GROUPS * len(PAIR_ORDER)
NEG = -1e30

V7X_VMEM_LIMIT = 56 * 1024 * 1024
ROW_TILE = 256
MOE_BLOCK = 128
BF16_ROWS = 16
ATTN_MAX_CHUNK = 1536
ATTN_MAX_BOUND = 40.0

f32 = jnp.float32
bf16 = jnp.bfloat16


def _params(*sem):
    return pltpu.CompilerParams(dimension_semantics=sem, vmem_limit_bytes=V7X_VMEM_LIMIT)


def _sigmoid(x):
    return 1.0 / (1.0 + jnp.exp(-x))


def _ada_kernel(cv_ref, w_ref, b_ref, o_ref):
    cv = cv_ref[...]
    s = (cv * _sigmoid(cv)).astype(bf16)
    o_ref[0] = jnp.dot(s, w_ref[0].astype(bf16), preferred_element_type=f32) + b_ref[0]


def _ada(cv, w_ada, b_ada):
    depth, d, n6 = w_ada.shape
    tn = min(512, n6)
    return pl.pallas_call(
        _ada_kernel,
        out_shape=jax.ShapeDtypeStruct((depth, 8, n6), f32),
        grid=(depth, n6 // tn),
        in_specs=[
            pl.BlockSpec((8, d), lambda l, j: (0, 0)),
            pl.BlockSpec((1, d, tn), lambda l, j: (l, 0, j)),
            pl.BlockSpec((1, 1, tn), lambda l, j: (l, 0, j)),
        ],
        out_specs=pl.BlockSpec((1, 8, tn), lambda l, j: (l, 0, j)),
        compiler_params=_params("arbitrary", "arbitrary"),
        name="ada",
    )(cv, w_ada, b_ada.reshape(depth, 1, n6))


def _norm_kernel(*refs, has_resid, has_norm):
    it = iter(refs)
    h_ref = next(it)
    if has_resid:
        f_ref, gate_ref = next(it), next(it)
    if has_norm:
        g_ref, shift_ref, scale_ref = next(it), next(it), next(it)
    h = h_ref[...]
    if has_resid:
        h = h + gate_ref[0] * f_ref[...]
        next(it)[...] = h
    if has_norm:
        ms = jnp.mean(h * h, axis=-1, keepdims=True)
        y = h * lax.rsqrt(ms + EPS) * g_ref[...]
        n_ref = next(it)
        n_ref[...] = (y * (1.0 + scale_ref[0]) + shift_ref[0]).astype(n_ref.dtype)


def _resid_norm(h, rows, seg_of, mods, resid=None, norm=None, n_dtype=bf16):
    t, d = h.shape
    tr = ROW_TILE
    row_spec = pl.BlockSpec((tr, d), lambda i: (i, 0))

    def mod_spec(layer, k):
        return pl.BlockSpec((1, 1, d), lambda i: ((layer * 8 + seg_of(i, tr)) * 6 + k, 0, 0))

    args, in_specs, out_shape, out_specs, aliases = [h], [row_spec], [], [], {}
    if resid is not None:
        f, layer, k_gate = resid
        args += [f, mods]
        in_specs += [row_spec, mod_spec(layer, k_gate)]
        out_shape.append(jax.ShapeDtypeStruct((t, d), f32))
        out_specs.append(row_spec)
        aliases = {0: 0}
    if norm is not None:
        g, layer, k_shift, k_scale = norm
        args += [g.reshape(1, d), mods, mods]
        in_specs += [pl.BlockSpec((1, d), lambda i: (0, 0)), mod_spec(layer, k_shift), mod_spec(layer, k_scale)]
        out_shape.append(jax.ShapeDtypeStruct((t, d), n_dtype))
        out_specs.append(row_spec)
    outs = pl.pallas_call(
        functools.partial(_norm_kernel, has_resid=resid is not None, has_norm=norm is not None),
        out_shape=out_shape,
        grid=(rows // tr,),
        in_specs=in_specs,
        out_specs=out_specs,
        input_output_aliases=aliases,
        compiler_params=_params("arbitrary"),
        name="resid_norm",
    )(*args)
    return outs


def _mm_kernel(a_ref, b_ref, o_ref):
    o_ref[...] = jnp.dot(a_ref[...], b_ref[...], preferred_element_type=f32).astype(o_ref.dtype)


def _matmul(a, b, layer, out_dtype, tm=512, tn=1024):
    m, k = a.shape
    n = b.shape[2]
    tm, tn = min(tm, m), math.gcd(tn, n)
    assert m % tm == 0 and tn % 128 == 0
    return pl.pallas_call(
        _mm_kernel,
        out_shape=jax.ShapeDtypeStruct((m, n), out_dtype),
        grid=(n // tn, m // tm),
        in_specs=[pl.BlockSpec((tm, k), lambda j, i: (i, 0)), pl.BlockSpec((None, k, tn), lambda j, i: (layer, 0, j))],
        out_specs=pl.BlockSpec((tm, tn), lambda j, i: (i, j)),
        compiler_params=_params("arbitrary", "arbitrary"),
        name="in_proj",
    )(a, b)


def _gates_kernel(n_ref, w_ref, b_ref, g_ref, c_ref, *, mh):
    tr = n_ref.shape[0]
    mg = jnp.dot(n_ref[...], w_ref[...], preferred_element_type=f32) + b_ref[...]
    g = GATE_CAP * jnp.tanh(mg * (1.0 / GATE_CAP))
    lane = lax.broadcasted_iota(jnp.int32, g.shape, 1)
    is_forget = (lane // mh) % 2 == 1
    log_sig = jnp.minimum(g, 0.0) - jnp.log(1.0 + jnp.exp(-jnp.abs(g)))
    val = jnp.where(is_forget, log_sig, g)
    g_ref[...] = val
    r = lax.broadcasted_iota(jnp.int32, (M_CHUNK, M_CHUNK), 0)
    c = lax.broadcasted_iota(jnp.int32, (M_CHUNK, M_CHUNK), 1)
    lower = (c <= r).astype(f32)
    upper = (c >= r).astype(f32)
    fwd = lax.broadcasted_iota(jnp.int32, (M_CHUNK, g.shape[1]), 1) < 2 * mh
    for ch in range(tr // M_CHUNK):
        v = val[ch * M_CHUNK:(ch + 1) * M_CHUNK]
        cf = jnp.dot(lower, v, preferred_element_type=f32, precision=lax.Precision.HIGHEST)
        cb = jnp.dot(upper, v, preferred_element_type=f32, precision=lax.Precision.HIGHEST)
        c_ref[ch * M_CHUNK:(ch + 1) * M_CHUNK, :] = jnp.where(fwd, cf, cb)


def _gates(n, w_g, layer, gate_b, mh):
    t, d = n.shape
    ng = 4 * mh
    tr = ROW_TILE
    spec = pl.BlockSpec((tr, ng), lambda i: (i, 0))
    return pl.pallas_call(
        functools.partial(_gates_kernel, mh=mh),
        out_shape=[jax.ShapeDtypeStruct((t, ng), f32)] * 2,
        grid=(t // tr,),
        in_specs=[pl.BlockSpec((tr, d), lambda i: (i, 0)), pl.BlockSpec((None, d, ng), lambda i: (layer, 0, 0)),
                  pl.BlockSpec((1, ng), lambda i: (0, 0))],
        out_specs=[spec, spec],
        compiler_params=_params("arbitrary"),
        name="gates",
    )(n, w_g, gate_b.reshape(1, ng).astype(f32))


def _qk_kernel(x_ref, g_ref, cos_ref, sin_ref, o_ref, *, hd):
    x = x_ref[...].astype(f32)
    cos, sin = cos_ref[...], sin_ref[...]
    r = lax.broadcasted_iota(jnp.int32, (hd, hd), 0)
    c = lax.broadcasted_iota(jnp.int32, (hd, hd), 1)
    swap = jnp.where(r == (c ^ (hd // 4)), 1.0, 0.0).astype(bf16)
    ones = jnp.ones((hd, hd), bf16)
    for s in range(x.shape[1] // hd):
        sl = slice(s * hd, (s + 1) * hd)
        xs = x[:, sl]
        ssq = jnp.dot((xs * xs).astype(bf16), ones, preferred_element_type=f32)
        y = xs * lax.rsqrt(ssq * (1.0 / hd) + EPS) * g_ref[:, sl]
        partner = jnp.dot(y.astype(bf16), swap, preferred_element_type=f32)
        o_ref[:, sl] = (y * cos + partner * sin).astype(o_ref.dtype)


def _qk_prep(parts, gain, cos_t, sin_t, width, n_x_tiles, tiles_per_seq, hd):
    t = parts.shape[0]
    tr = ROW_TILE
    wq = min(512, width)

    def tbl(i, j):
        return (jnp.where(i < n_x_tiles, i % tiles_per_seq, tiles_per_seq), 0)

    return pl.pallas_call(
        functools.partial(_qk_kernel, hd=hd),
        out_shape=jax.ShapeDtypeStruct((t, width), bf16),
        grid=(t // tr, width // wq),
        in_specs=[pl.BlockSpec((tr, wq), lambda i, j: (i, j)), pl.BlockSpec((1, wq), lambda i, j: (0, j)),
                  pl.BlockSpec((tr, hd), tbl), pl.BlockSpec((tr, hd), tbl)],
        out_specs=pl.BlockSpec((tr, wq), lambda i, j: (i, j)),
        compiler_params=_params("arbitrary", "arbitrary"),
        name="qk_prep",
    )(parts, gain, cos_t, sin_t)


def _attn_kernel(q_ref, kx_ref, kc_ref, vx_ref, vc_ref, lam_ref, bound_ref, g_ref, o_ref, k_ref, vt_ref, *, hd, tkv,
                 tt, nq_x, with_ctx_queries, score_bound):
    i = pl.program_id(2)
    tq = q_ref.shape[0]
    l_ctx, s = kc_ref.shape[0], kx_ref.shape[0]

    @pl.when(i == 0)
    def _():
        k_ref[0:l_ctx] = kc_ref[...]
        vt_ref[:, 0:l_ctx] = vc_ref[...].T
        for c in range(s // tt):
            k_ref[l_ctx + c * tt:l_ctx + (c + 1) * tt] = kx_ref[c * tt:(c + 1) * tt]
            vt_ref[:, l_ctx + c * tt:l_ctx + (c + 1) * tt] = vx_ref[c * tt:(c + 1) * tt].T

    def attend(n_keys, tk):
        qt = q_ref[...].T
        qt0, qt1 = qt[:hd], qt[hd:]

        def scores(c):
            k = k_ref[c * tk:(c + 1) * tk]
            return jnp.concatenate([jnp.dot(k[:, :hd], qt0, preferred_element_type=f32),
                                    jnp.dot(k[:, hd:], qt1, preferred_element_type=f32)], axis=1)

        m = l_sum = acc = None
        st = scores(0)
        for c in range(n_keys // tk):
            st_next = scores(c + 1) if (c + 1) * tk < n_keys else None
            if score_bound:
                m_new = bound_ref[...]
            else:
                m_c = jnp.max(st, axis=0, keepdims=True)
                m_new = m_c if m is None else jnp.maximum(m, m_c)
            p = jnp.exp(st - m_new)
            pv = jnp.dot(vt_ref[:, c * tk:(c + 1) * tk], p.astype(bf16), preferred_element_type=f32)
            if m is None:
                l_sum, acc = jnp.sum(p, axis=0, keepdims=True), pv
            elif score_bound:
                l_sum, acc = l_sum + jnp.sum(p, axis=0, keepdims=True), acc + pv
            else:
                alpha = jnp.exp(m - m_new)
                l_sum, acc = alpha * l_sum + jnp.sum(p, axis=0, keepdims=True), alpha * acc + pv
            m, st = m_new, st_next
        acc = acc / l_sum
        a = (acc[:, :tq] - lam_ref[...] * acc[:, tq:]).T
        ms = jnp.mean(a * a, axis=-1, keepdims=True)
        o_ref[...] = (a * lax.rsqrt(ms + EPS) * g_ref[...]).astype(o_ref.dtype)

    if with_ctx_queries:
        pl.when(i < nq_x)(lambda: attend(l_ctx + s, tkv))
        pl.when(i >= nq_x)(lambda: attend(l_ctx, l_ctx))
    else:
        attend(l_ctx + s, tkv)


def _attention(qk2, parts, lam, bound, out_gain, dims, with_ctx_queries):
    b, s, l, heads, hd = dims
    a_w = heads * 2 * hd
    t = qk2.shape[0]
    tq = ROW_TILE
    n_keys = s + l
    tkv = next(n_keys // n for n in range(1, n_keys) if n_keys % (n * 128) == 0 and n_keys // n <= ATTN_MAX_CHUNK)
    nq_x, nq_c = s // tq, l // tq
    nq = nq_x + (nq_c if with_ctx_queries else 0)
    hw = 2 * hd
    k_col, v_col = a_w // hw, 2 * a_w // hw
    ctx_row = b * s // l

    def q_idx(bi, h, i):
        return (jnp.where(i < nq_x, bi * nq_x + i, b * nq_x + bi * nq_c + (i - nq_x)), h)

    def call(score_bound):
        return pl.pallas_call(
            functools.partial(_attn_kernel, hd=hd, tkv=tkv, tt=min(512, s), nq_x=nq_x,
                              with_ctx_queries=with_ctx_queries, score_bound=score_bound),
            out_shape=jax.ShapeDtypeStruct((t if with_ctx_queries else b * s, a_w), bf16),
            grid=(b, heads, nq),
            in_specs=[
                pl.BlockSpec((tq, hw), q_idx),
                pl.BlockSpec((s, hw), lambda bi, h, i: (bi, k_col + h)),
                pl.BlockSpec((l, hw), lambda bi, h, i: (ctx_row + bi, k_col + h)),
                pl.BlockSpec((s, hw), lambda bi, h, i: (bi, v_col + h)),
                pl.BlockSpec((l, hw), lambda bi, h, i: (ctx_row + bi, v_col + h)),
                pl.BlockSpec((1, 1), lambda bi, h, i: (0, 0)),
                pl.BlockSpec((1, 1), lambda bi, h, i: (0, 0)),
                pl.BlockSpec((1, hw), lambda bi, h, i: (0, 0)),
            ],
            out_specs=pl.BlockSpec((tq, hw), q_idx),
            scratch_shapes=[pltpu.VMEM((n_keys, hw), bf16), pltpu.VMEM((hw, n_keys), bf16)],
            compiler_params=_params("arbitrary", "arbitrary", "arbitrary"),
            name="diff_attn",
        )(qk2, qk2, qk2, parts, parts, lam, bound, out_gain)

    return lax.cond(bound[0, 0] <= ATTN_MAX_BOUND, lambda: call(True), lambda: call(False))


def _mprep_kernel(x_ref, p_ref, n_ref, w_ref, b_ref, sc_ref, o_ref, *, s, l, bs):
    tr = x_ref.shape[0]
    row0 = pl.program_id(0) * tr
    in_x = row0 < bs
    rel = jnp.where(in_x, row0 % s, (row0 - bs) % l)
    seq_len = jnp.where(in_x, s, l)
    keep_prev = jnp.where(rel == 0, 0.0, 1.0)
    keep_next = jnp.where(rel + tr == seq_len, 0.0, 1.0)
    x = x_ref[...].astype(f32)
    prev = p_ref[...].astype(f32)[BF16_ROWS - 1:BF16_ROWS] * keep_prev
    nxt = n_ref[...].astype(f32)[0:1] * keep_next
    rows = lax.broadcasted_iota(jnp.int32, x.shape, 0)
    xm = jnp.where(rows == 0, prev, pltpu.roll(x, 1, 0))
    xp = jnp.where(rows == tr - 1, nxt, pltpu.roll(x, tr - 1, 0))
    w = w_ref[...]
    u = b_ref[...] + xm * w[0:1] + x * w[1:2] + xp * w[2:3]
    o_ref[...] = (u * _sigmoid(u) * sc_ref[...]).astype(o_ref.dtype)


def _mlstm_prep(parts, col0, conv_w, conv_b, post, dims):
    b, s, l = dims
    t = parts.shape[0]
    width = conv_w.shape[1]
    tr = ROW_TILE
    wc = min(512, width)
    cb = col0 // wc
    sub = tr // BF16_ROWS
    last = t // BF16_ROWS - 1
    return pl.pallas_call(
        functools.partial(_mprep_kernel, s=s, l=l, bs=b * s),
        out_shape=jax.ShapeDtypeStruct((t, width), bf16),
        grid=(t // tr, width // wc),
        in_specs=[
            pl.BlockSpec((tr, wc), lambda i, j: (i, cb + j)),
            pl.BlockSpec((BF16_ROWS, wc), lambda i, j: (jnp.maximum(i * sub - 1, 0), cb + j)),
            pl.BlockSpec((BF16_ROWS, wc), lambda i, j: (jnp.minimum((i + 1) * sub, last), cb + j)),
            pl.BlockSpec((M_CONV, wc), lambda i, j: (0, j)),
            pl.BlockSpec((1, wc), lambda i, j: (0, j)),
            pl.BlockSpec((1, wc), lambda i, j: (0, j)),
        ],
        out_specs=pl.BlockSpec((tr, wc), lambda i, j: (i, j)),
        compiler_params=_params("arbitrary", "arbitrary"),
        name="mlstm_prep",
    )(parts, parts, parts, conv_w, conv_b, post)


def _scan_chunk(d, h, mh, q, k, v, gcol, grow_ref, c_st, n_st, m_st):
    ng = 4 * mh
    col_i = d * 2 * mh + h
    col_f = ng + col_i + mh
    lane = lax.broadcasted_iota(jnp.int32, gcol.shape, 1)
    li_col = jnp.sum(jnp.where(lane == col_i, gcol, 0.0), axis=1, keepdims=True)
    bs_col = jnp.sum(jnp.where(lane == col_f, gcol, 0.0), axis=1, keepdims=True)
    li_row = grow_ref[pl.ds(col_i, 1), :]
    bs_row = grow_ref[pl.ds(col_f, 1), :]
    r = lax.broadcasted_iota(jnp.int32, (M_CHUNK, M_CHUNK), 0)
    c = lax.broadcasted_iota(jnp.int32, (M_CHUNK, M_CHUNK), 1)
    visible = (c <= r) if d == 0 else (c >= r)
    last = M_CHUNK - 1 if d == 0 else 0
    b_end = bs_row[:, last:last + 1]

    d_log = jnp.where(visible, bs_col + (li_row - bs_row), NEG)
    inter = bs_col + m_st
    m_pos = jnp.maximum(jnp.max(d_log, axis=1, keepdims=True), inter)
    w_intra = jnp.exp(d_log - m_pos)
    w_inter = jnp.exp(inter - m_pos)
    sc = lax.dot_general(q, k, (((1,), (1,)), ((), ())), preferred_element_type=f32) * w_intra
    num = jnp.dot(sc.astype(bf16), v, preferred_element_type=f32) + w_inter * jnp.dot(
        q, c_st.astype(bf16), preferred_element_type=f32)
    qn = jnp.sum(q.astype(f32) * n_st, axis=1, keepdims=True)
    den = jnp.sum(sc, axis=1, keepdims=True) + w_inter * qn
    hidden = num / jnp.maximum(jnp.abs(den), jnp.exp(-m_pos))

    w_log = b_end - bs_col + li_col
    m_new = jnp.maximum(b_end + m_st, jnp.max(w_log, axis=0, keepdims=True))
    decay = jnp.exp(b_end + m_st - m_new)
    wk = jnp.exp(w_log - m_new) * k.astype(f32)
    c_new = decay * c_st + lax.dot_general(wk.astype(bf16), v, (((0,), (0,)), ((), ())),
                                           preferred_element_type=f32)
    n_new = decay * n_st + jnp.sum(wk, axis=0, keepdims=True)
    return hidden, c_new, n_new, m_new


def _scan_kernel(*refs, mh):
    h = pl.program_id(1)
    ins, outs, (c_ref, n_ref, m_ref) = refs[:10], refs[10:12], refs[12:]

    @pl.when(pl.program_id(2) == 0)
    def _():
        for ref in (c_ref, n_ref, m_ref):
            ref[...] = jnp.zeros(ref.shape, f32)

    res = []
    for d in range(2):
        q_ref, k_ref, v_ref, gcol_ref, grow_ref = ins[5 * d:5 * d + 5]
        res.append(_scan_chunk(d, h, mh, q_ref[...], k_ref[...], v_ref[...], gcol_ref[...], grow_ref,
                               c_ref[d], n_ref[d], m_ref[d]))
    for d in range(2):
        outs[d][...] = res[d][0].astype(outs[d].dtype)
    c_ref[...] = jnp.stack([res[0][1], res[1][1]])
    n_ref[...] = jnp.stack([res[0][2], res[1][2]])
    m_ref[...] = jnp.stack([res[0][3], res[1][3]])


def _mlstm_scan(mqk, parts, v_col0, gc_col, gc_row, dims):
    b, s, l, mh, dk, dv = dims
    t = mqk.shape[0]
    n_lc, n_sc = l // M_CHUNK, s // M_CHUNK
    ctx0 = b * s // M_CHUNK
    vb = v_col0 // dv
    ng2 = gc_col.shape[1]

    def chunk(d, bi, st):
        in_ctx = st < n_lc
        local = jnp.where(in_ctx, st, st - n_lc)
        n_local = jnp.where(in_ctx, n_lc, n_sc)
        pos = local if d == 0 else n_local - 1 - local
        return jnp.where(in_ctx, ctx0 + bi * n_lc, bi * n_sc) + pos

    def dir_specs(d):
        return [
            pl.BlockSpec((M_CHUNK, dk), lambda bi, h, st: (chunk(d, bi, st), h)),
            pl.BlockSpec((M_CHUNK, dk), lambda bi, h, st: (chunk(d, bi, st), mh + h)),
            pl.BlockSpec((M_CHUNK, dv), lambda bi, h, st: (chunk(d, bi, st), vb + h)),
            pl.BlockSpec((M_CHUNK, ng2), lambda bi, h, st: (chunk(d, bi, st), 0)),
            pl.BlockSpec((ng2, M_CHUNK), lambda bi, h, st: (0, chunk(d, bi, st))),
        ]

    def out_spec(d):
        return pl.BlockSpec((M_CHUNK, dv), lambda bi, h, st: (chunk(d, bi, st), h))

    state = [pltpu.VMEM((2, dk, dv), f32), pltpu.VMEM((2, 1, dk), f32), pltpu.VMEM((2, 1, 1), f32)]
    args = (mqk, mqk, parts, gc_col, gc_row)
    return pl.pallas_call(
        functools.partial(_scan_kernel, mh=mh),
        out_shape=[jax.ShapeDtypeStruct((t, mh * dv), bf16)] * 2,
        grid=(b, mh, n_lc + n_sc),
        in_specs=dir_specs(0) + dir_specs(1),
        out_specs=[out_spec(0), out_spec(1)],
        scratch_shapes=state,
        compiler_params=_params("arbitrary", "arbitrary", "arbitrary"),
        name="mlstm_scan",
    )(*args, *args)


def _mout_kernel(hf_ref, hb_ref, o_ref, g_ref, out_ref):
    hs = hf_ref[...].astype(f32) + hb_ref[...].astype(f32)
    ms = jnp.mean(hs * hs, axis=-1, keepdims=True)
    y = hs * lax.rsqrt(ms + EPS) * g_ref[0]
    out_ref[...] = (y * _sigmoid(o_ref[...].astype(f32))).astype(out_ref.dtype)


def _mlstm_out(h_fwd, h_bwd, parts, o_col0, head_g, rows):
    m_w = h_fwd.shape[1]
    mh, dv = head_g.shape
    tr = ROW_TILE
    ob = o_col0 // dv
    head_spec = pl.BlockSpec((tr, dv), lambda i, h: (i, h))
    return pl.pallas_call(
        _mout_kernel,
        out_shape=jax.ShapeDtypeStruct((rows, m_w), bf16),
        grid=(rows // tr, mh),
        in_specs=[head_spec, head_spec, pl.BlockSpec((tr, dv), lambda i, h: (i, ob + h)),
                  pl.BlockSpec((1, 1, dv), lambda i, h: (h, 0, 0))],
        out_specs=head_spec,
        compiler_params=_params("arbitrary", "arbitrary"),
        name="mlstm_out",
    )(h_fwd, h_bwd, parts, head_g.reshape(mh, 1, dv).astype(f32))


def _merge_kernel(a_ref, wa_ref, m_ref, wm_ref, ga_ref, gb_ref, o_ref):
    ya = jnp.dot(a_ref[...], wa_ref[...], preferred_element_type=f32)
    ym = jnp.dot(m_ref[...], wm_ref[...], preferred_element_type=f32)
    y = _sigmoid(ga_ref[...].astype(f32)) * ya + _sigmoid(gb_ref[...].astype(f32)) * ym
    o_ref[...] = y.astype(o_ref.dtype)


def _merge(a_flat, m_flat, w_a, w_m, layer, parts, ga_col0, rows):
    d = w_a.shape[2]
    tm, tn = 512, min(512, d)
    ga, gb = ga_col0 // tn, (ga_col0 + d) // tn
    return pl.pallas_call(
        _merge_kernel,
        out_shape=jax.ShapeDtypeStruct((rows, d), bf16),
        grid=(d // tn, rows // tm),
        in_specs=[
            pl.BlockSpec((tm, a_flat.shape[1]), lambda j, i: (i, 0)),
            pl.BlockSpec((None, w_a.shape[1], tn), lambda j, i: (layer, 0, j)),
            pl.BlockSpec((tm, m_flat.shape[1]), lambda j, i: (i, 0)),
            pl.BlockSpec((None, w_m.shape[1], tn), lambda j, i: (layer, 0, j)),
            pl.BlockSpec((tm, tn), lambda j, i: (i, ga + j)),
            pl.BlockSpec((tm, tn), lambda j, i: (i, gb + j)),
        ],
        out_specs=pl.BlockSpec((tm, tn), lambda j, i: (i, j)),
        compiler_params=_params("arbitrary", "arbitrary"),
        name="branch_merge",
    )(a_flat, w_a, m_flat, w_m, parts, parts)


def _outproj_kernel(y_ref, w_ref, h_ref, gate_ref, o_ref):
    o_ref[...] = h_ref[...] + gate_ref[0] * jnp.dot(y_ref[...], w_ref[...], preferred_element_type=f32)


def _out_proj(y, w_out, h, mods, layer, seg_of, rows):
    t, d = h.shape
    tm, tn = 512, min(512, d)
    return pl.pallas_call(
        _outproj_kernel,
        out_shape=jax.ShapeDtypeStruct((t, d), f32),
        grid=(d // tn, rows // tm),
        in_specs=[
            pl.BlockSpec((tm, d), lambda j, i: (i, 0)),
            pl.BlockSpec((None, d, tn), lambda j, i: (layer, 0, j)),
            pl.BlockSpec((tm, tn), lambda j, i: (i, j)),
            pl.BlockSpec((1, 1, tn), lambda j, i: ((layer * 8 + seg_of(i, tm)) * 6 + 2, 0, j)),
        ],
        out_specs=pl.BlockSpec((tm, tn), lambda j, i: (i, j)),
        input_output_aliases={2: 0},
        compiler_params=_params("arbitrary", "arbitrary"),
        name="out_proj",
    )(y, w_out, h, mods)


def _route_kernel(h_ref, g_ref, shift_ref, scale_ref, wr_ref, bias_ref, n_ref, r_ref):
    h = h_ref[...]
    ms = jnp.mean(h * h, axis=-1, keepdims=True)
    n = h * lax.rsqrt(ms + EPS) * g_ref[...] * (1.0 + scale_ref[0]) + shift_ref[0]
    n_ref[...] = n
    logits = lax.dot_general(wr_ref[...], n, (((1,), (1,)), ((), ())), preferred_element_type=f32,
                             precision=lax.Precision.HIGHEST)
    score = _sigmoid(logits)
    sel = score + bias_ref[...]
    sel_e = [sel[e:e + 1] for e in range(N_GROUPS * GROUP_SIZE)]
    score_e = [score[e:e + 1] for e in range(N_GROUPS * GROUP_SIZE)]

    def top2_sum(v):
        best = v[0] + v[1]
        for a in range(GROUP_SIZE):
            for b in range(a + 1, GROUP_SIZE):
                if (a, b) != (0, 1):
                    best = jnp.maximum(best, v[a] + v[b])
        return best

    g_best = top2_sum(sel_e[:GROUP_SIZE])
    grp = jnp.zeros(g_best.shape, jnp.int32)
    for g in range(1, N_GROUPS):
        cand = top2_sum(sel_e[g * GROUP_SIZE:(g + 1) * GROUP_SIZE])
        better = cand > g_best
        grp = jnp.where(better, g, grp)
        g_best = jnp.where(better, cand, g_best)

    def in_group(rows):
        out = []
        for j in range(GROUP_SIZE):
            v = rows[j]
            for g in range(1, N_GROUPS):
                v = jnp.where(grp == g, rows[g * GROUP_SIZE + j], v)
            out.append(v)
        return out

    v, sc = in_group(sel_e), in_group(score_e)
    j1, b1, c1 = jnp.zeros(grp.shape, jnp.int32), v[0], sc[0]
    for j in range(1, GROUP_SIZE):
        better = v[j] > b1
        j1, b1, c1 = jnp.where(better, j, j1), jnp.where(better, v[j], b1), jnp.where(better, sc[j], c1)
    j2 = jnp.full(grp.shape, -1, jnp.int32)
    b2, c2 = jnp.full(b1.shape, -jnp.inf, f32), jnp.zeros(b1.shape, f32)
    for j in range(GROUP_SIZE):
        better = (j1 != j) & ((v[j] > b2) | (j2 < 0))
        j2, b2, c2 = jnp.where(better, j, j2), jnp.where(better, v[j], b2), jnp.where(better, sc[j], c2)
    tot = c1 + c2
    w1, w2 = c1 / tot, c2 / tot
    first_low = j1 < j2
    lo, hi = jnp.minimum(j1, j2), jnp.maximum(j1, j2)
    pair_in_group = jnp.zeros(lo.shape, jnp.int32)
    for idx, (p_lo, p_hi) in enumerate(PAIR_ORDER):
        pair_in_group = jnp.where((lo == p_lo) & (hi == p_hi), idx, pair_in_group)
    r_ref[...] = jnp.zeros(r_ref.shape, f32)
    r_ref[0:1, :] = (grp * 6 + pair_in_group).astype(f32)
    r_ref[1:2, :] = jnp.where(first_low, w1, w2)
    r_ref[2:3, :] = jnp.where(first_low, w2, w1)


def _norm_route(h, g, mods, layer, seg_of, w_router_t, bias, rows):
    t, d = h.shape
    e = w_router_t.shape[0]
    tr = ROW_TILE

    def mod_spec(k):
        return pl.BlockSpec((1, 1, d), lambda i: ((layer * 8 + seg_of(i, tr)) * 6 + k, 0, 0))

    return pl.pallas_call(
        _route_kernel,
        out_shape=[jax.ShapeDtypeStruct((rows, d), f32), jax.ShapeDtypeStruct((8, rows), f32)],
        grid=(rows // tr,),
        in_specs=[pl.BlockSpec((tr, d), lambda i: (i, 0)), pl.BlockSpec((1, d), lambda i: (0, 0)),
                  mod_spec(3), mod_spec(4), pl.BlockSpec((e, d), lambda i: (0, 0)),
                  pl.BlockSpec((e, 1), lambda i: (0, 0))],
        out_specs=[pl.BlockSpec((tr, d), lambda i: (i, 0)), pl.BlockSpec((8, tr), lambda i: (0, i))],
        compiler_params=_params("arbitrary"),
        name="norm_route",
    )(h, g.reshape(1, d), mods, mods, w_router_t, bias.reshape(e, 1).astype(f32))


def _moe_kernel(lo_ref, hi_ref, nv_ref, tok_ref, x_hbm, w_ref, wg_lo, wu_lo, wd_lo, wg_hi, wu_hi, wd_hi,
                out_hbm, xbuf, ybuf, sem_in, sem_out):
    blk = pl.program_id(0)
    slot = blk % 2

    def row_in(b, r):
        return pltpu.make_async_copy(x_hbm.at[pl.ds(tok_ref[b * MOE_BLOCK + r], 1)],
                                     xbuf.at[b % 2, pl.ds(r, 1)], sem_in.at[b % 2])

    def row_out(b, r):
        return pltpu.make_async_copy(ybuf.at[b % 2, pl.ds(r, 1)],
                                     out_hbm.at[pl.ds(tok_ref[b * MOE_BLOCK + r], 1)], sem_out.at[b % 2])

    def for_rows(fn):
        def body(r, c):
            fn(r)
            return c

        lax.fori_loop(0, MOE_BLOCK, body, 0, unroll=8)

    def start_out(b):
        for_rows(lambda r: pl.when(r < nv_ref[b])(lambda: row_out(b, r).start()))

    def wait_in(b):
        pltpu.make_async_copy(x_hbm.at[pl.ds(0, MOE_BLOCK)], xbuf.at[b % 2], sem_in.at[b % 2]).wait()

    def wait_out(b):
        nv = nv_ref[b]
        p = 1
        while p <= MOE_BLOCK:
            rows = pl.ds(0, p)
            pl.when((nv & p) != 0)(
                pltpu.make_async_copy(ybuf.at[b % 2, rows], out_hbm.at[rows], sem_out.at[b % 2]).wait)
            p *= 2

    def has_rows(b):
        n_blk = pl.num_programs(0)
        return jnp.logical_and(b < n_blk, nv_ref[jnp.minimum(b, n_blk - 1)] > 0)

    @pl.when(nv_ref[blk] > 0)
    def _():
        @pl.when(blk == 0)
        def _():
            for_rows(lambda r: row_in(blk, r).start())

        wait_in(blk)

        @pl.when(has_rows(blk + 1))
        def _():
            for_rows(lambda r: row_in(blk + 1, r).start())

        x = xbuf[slot].astype(bf16)
        eye = (lax.broadcasted_iota(jnp.int32, (MOE_BLOCK, MOE_BLOCK), 0)
               == lax.broadcasted_iota(jnp.int32, (MOE_BLOCK, MOE_BLOCK), 1))
        w = w_ref[0]

        def expert(wg, wu, wd, w_row):
            a = jnp.dot(x, wg[0], preferred_element_type=f32)
            u = jnp.dot(x, wu[0], preferred_element_type=f32)
            y = jnp.dot((a * _sigmoid(a) * u).astype(bf16), wd[0], preferred_element_type=f32)
            return y * jnp.sum(jnp.where(eye, w_row, 0.0), axis=1, keepdims=True)

        ybuf[slot] = expert(wg_lo, wu_lo, wd_lo, w[0:1]) + expert(wg_hi, wu_hi, wd_hi, w[1:2])

        @pl.when(blk > 0)
        def _():
            wait_out(blk - 1)

        start_out(blk)

        @pl.when(jnp.logical_not(has_rows(blk + 1)))
        def _():
            wait_out(blk)


def _moe(n2, route, w_gate, w_up, w_down, layer, n_tok):
    d = n2.shape[1]
    dff = w_gate.shape[-1]
    blk = MOE_BLOCK
    n_blk = -(-n_tok // blk) + N_PAIRS
    n_pad = n_blk * blk
    pair = route[0, :n_tok].astype(jnp.int32)
    onehot = (pair[:, None] == jnp.arange(N_PAIRS, dtype=jnp.int32)[None, :]).astype(jnp.int32)
    csum = jnp.cumsum(onehot, axis=0)
    rank = jnp.sum(csum * onehot, axis=1) - 1
    counts = csum[-1]
    padded = (counts + blk - 1) // blk * blk
    pad_end = jnp.cumsum(padded)
    pad_start = pad_end - padded
    dest = pad_start[pair] + rank
    buf_tok = jnp.zeros((n_pad,), jnp.int32).at[dest].set(jnp.arange(n_tok, dtype=jnp.int32))
    wts = jnp.zeros((2, n_pad), f32).at[:, dest].set(route[1:3, :n_tok])
    wts = wts.reshape(2, n_blk, blk).transpose(1, 0, 2)
    blk_start = jnp.arange(n_blk, dtype=jnp.int32) * blk
    seg = jnp.minimum(jnp.searchsorted(pad_end, blk_start, side="right"), N_PAIRS - 1).astype(jnp.int32)
    nv = jnp.clip(counts[seg] - (blk_start - pad_start[seg]), 0, blk).astype(jnp.int32)
    pair_tbl = jnp.asarray(PAIR_ORDER, jnp.int32)
    e_lo = (seg // len(PAIR_ORDER)) * GROUP_SIZE + pair_tbl[seg % len(PAIR_ORDER), 0]
    e_hi = (seg // len(PAIR_ORDER)) * GROUP_SIZE + pair_tbl[seg % len(PAIR_ORDER), 1]

    def w_in(sel):
        return pl.BlockSpec((None, 1, d, dff), lambda i, lo, hi, nvr, tok: (layer, (lo, hi)[sel][i], 0, 0),
                            pipeline_mode=pl.Buffered(1))

    def w_out(sel):
        return pl.BlockSpec((None, 1, dff, d), lambda i, lo, hi, nvr, tok: (layer, (lo, hi)[sel][i], 0, 0),
                            pipeline_mode=pl.Buffered(1))

    return pl.pallas_call(
        _moe_kernel,
        out_shape=jax.ShapeDtypeStruct((n_tok, d), f32),
        grid_spec=pltpu.PrefetchScalarGridSpec(
            num_scalar_prefetch=4,
            grid=(n_blk,),
            in_specs=[pl.BlockSpec(memory_space=pl.ANY),
                      pl.BlockSpec((1, 2, blk), lambda i, lo, hi, nvr, tok: (i, 0, 0)),
                      w_in(0), w_in(0), w_out(0), w_in(1), w_in(1), w_out(1)],
            out_specs=pl.BlockSpec(memory_space=pl.ANY),
            scratch_shapes=[pltpu.VMEM((2, blk, d), f32), pltpu.VMEM((2, blk, d), f32),
                            pltpu.SemaphoreType.DMA((2,)), pltpu.SemaphoreType.DMA((2,))],
        ),
        compiler_params=_params("arbitrary"),
        name="moe",
    )(e_lo, e_hi, nv, buf_tok, n2, wts, w_gate, w_up, w_down, w_gate, w_up, w_down)


def _rope_tables(n_rows, hd, pad_rows):
    n_freq = hd // 4
    inv_freq = ROPE_THETA ** (-jnp.arange(n_freq, dtype=f32) / n_freq)
    pos = jnp.arange(n_rows * GRID_W)
    ang_row = (pos // GRID_W).astype(f32)[:, None] * inv_freq
    ang_col = (pos % GRID_W).astype(f32)[:, None] * inv_freq
    ang = jnp.concatenate([ang_row, ang_row, ang_col, ang_col], axis=1)
    sign = jnp.tile(jnp.concatenate([-jnp.ones((n_freq,), f32), jnp.ones((n_freq,), f32)]), 2)
    cos_t = jnp.concatenate([jnp.cos(ang), jnp.ones((pad_rows, hd), f32)], axis=0)
    sin_t = jnp.concatenate([jnp.sin(ang) * sign, jnp.zeros((pad_rows, hd), f32)], axis=0)
    return cos_t, sin_t


def kernel(x, c, ctx, c_ctx, w_ada, b_ada, g_norm1, g_norm2, w_in, q_norm_g, k_norm_g, diff_lambda, attn_head_g,
           conv_w, conv_b, mlstm_gate_b, mlstm_head_g, w_br_attn, w_br_mlstm, w_out, w_router, router_bias,
           w_gate, w_up, w_down):
    b, s, d = x.shape
    l = ctx.shape[1]
    depth = w_ada.shape[0]
    hd = q_norm_g.shape[-1]
    a_w = w_br_attn.shape[1]
    heads = a_w // (2 * hd)
    mh, dv = mlstm_head_g.shape[-2:]
    qk_w = conv_w.shape[-1] // 2
    dk = qk_w // mh
    m_w = mh * dv
    ng = 4 * mh
    assert w_router.shape[1] == N_GROUPS * GROUP_SIZE
    assert s % ROW_TILE == 0 and l % ROW_TILE == 0 and s % GRID_W == 0 and b + 1 <= 8
    t_x, t_all = b * s, b * s + b * l
    gate_col = 3 * a_w + 2 * qk_w + 2 * m_w
    mq_col, mv_col, mo_col, ga_col = 3 * a_w, 3 * a_w + 2 * qk_w, 3 * a_w + 2 * qk_w + m_w, gate_col

    def seg_of(i, tile):
        return jnp.where(i < t_x // tile, i // (s // tile), b)

    w_main = jnp.concatenate([w_in[:, :, :gate_col], w_in[:, :, gate_col + ng:]], axis=2).astype(bf16)
    w_g = w_in[:, :, gate_col:gate_col + ng].astype(bf16)
    w_a, w_m, w_o = w_br_attn.astype(bf16), w_br_mlstm.astype(bf16), w_out.astype(bf16)
    w_eg, w_eu, w_ed = w_gate.astype(bf16), w_up.astype(bf16), w_down.astype(bf16)
    cv = jnp.zeros((8, d), f32).at[:b].set(c).at[b].set(c_ctx)
    mods = _ada(cv, w_ada, b_ada).reshape(depth * 8 * 6, 1, d)
    cos_t, sin_t = _rope_tables(s // GRID_W, hd, ROW_TILE)
    h = jnp.concatenate([x.reshape(t_x, d), ctx.reshape(b * l, d)], axis=0)
    f = None
    for layer in range(depth):
        need_ctx = layer < depth - 1
        rows = t_all if need_ctx else t_x
        lam_init = 0.8 - 0.6 * math.exp(-0.3 * layer)
        lv = diff_lambda[layer].astype(f32)
        lam = (jnp.exp(jnp.sum(lv[0] * lv[1])) - jnp.exp(jnp.sum(lv[2] * lv[3])) + lam_init).reshape(1, 1)
        if f is None:
            (n1,) = _resid_norm(h, t_all, seg_of, mods, norm=(g_norm1[layer], layer, 0, 1))
        else:
            h, n1 = _resid_norm(h, t_all, seg_of, mods, resid=(f, layer - 1, 5), norm=(g_norm1[layer], layer, 0, 1))
        parts = _matmul(n1, w_main, layer, bf16)
        g_col, c_col = _gates(n1, w_g, layer, mlstm_gate_b[layer], mh)

        qk_gain = jnp.concatenate([jnp.tile(q_norm_g[layer].astype(f32), 2 * heads) * hd ** -0.5,
                                   jnp.tile(k_norm_g[layer].astype(f32), 2 * heads)]).reshape(1, 2 * a_w)
        qk2 = _qk_prep(parts, qk_gain, cos_t, sin_t, 2 * a_w, t_x // ROW_TILE, s // ROW_TILE, hd)
        out_gain = (attn_head_g[layer].astype(f32) * (1.0 - lam_init)).reshape(1, 2 * hd)
        bound = (1.01 * hd ** 0.5 * jnp.max(jnp.abs(q_norm_g[layer])) * jnp.max(jnp.abs(k_norm_g[layer])))
        a_flat = _attention(qk2, parts, lam, bound.astype(f32).reshape(1, 1), out_gain, (b, s, l, heads, hd), need_ctx)

        post = jnp.concatenate([jnp.ones((qk_w,), f32), jnp.full((qk_w,), dk ** -0.5, f32)]).reshape(1, 2 * qk_w)
        mqk = _mlstm_prep(parts, mq_col, conv_w[layer].astype(f32), conv_b[layer].reshape(1, -1).astype(f32), post,
                          (b, s, l))
        gc_col = jnp.concatenate([g_col, c_col], axis=1)
        h_fwd, h_bwd = _mlstm_scan(mqk, parts, mv_col, gc_col, gc_col.T, (b, s, l, mh, dk, dv))
        m_flat = _mlstm_out(h_fwd, h_bwd, parts, mo_col, mlstm_head_g[layer], rows)

        y = _merge(a_flat, m_flat, w_a, w_m, layer, parts, ga_col, rows)
        h = _out_proj(y, w_o, h, mods, layer, seg_of, rows)
        n2, route = _norm_route(h, g_norm2[layer], mods, layer, seg_of, w_router.T.astype(f32), router_bias, rows)
        f = _moe(n2, route, w_eg, w_eu, w_ed, layer, rows)
    (h,) = _resid_norm(h, t_x, seg_of, mods, resid=(f, depth - 1, 5))
    return h[:t_x].reshape(b, s, d)
```

```python
import functools
import math

import jax
import jax.numpy as jnp
from jax import lax
from jax.experimental import pallas as pl
from jax.experimental.pallas import tpu as pltpu

GRID_W = 64
EPS = 1e-6
ROPE_THETA = 10000.0
M_CHUNK = 128
M_CONV = 3
GATE_CAP = 15.0
N_GROUPS = 4
GROUP_SIZE = 4
PAIR_ORDER = ((0, 1), (0, 2), (1, 2), (1, 3), (0, 3), (2, 3))
N_PAIRS = N_GROUPS * len(PAIR_ORDER)
NEG = -1e30

V7X_VMEM_LIMIT = 56 * 1024 * 1024
ROW_TILE = 256
MOE_BLOCK = 128
BF16_ROWS = 16
SCAN_SUB = 2
SCAN_HEADS = 2
ATTN_MAX_CHUNK = 1536
ATTN_MAX_BOUND = 40.0

f32 = jnp.float32
bf16 = jnp.bfloat16


def _params(*sem):
    return pltpu.CompilerParams(dimension_semantics=sem, vmem_limit_bytes=V7X_VMEM_LIMIT)


def _sigmoid(x):
    return 1.0 / (1.0 + jnp.exp(-x))


def _ada_kernel(cvt_ref, w_ref, b_ref, o_ref, *, n_rows):
    cvt = cvt_ref[...]
    s = cvt * _sigmoid(cvt)
    w = w_ref[0]
    o_ref[0] = jnp.zeros(o_ref.shape[1:], f32)
    for r in range(n_rows):
        o_ref[0, r:r + 1, :] = jnp.sum(w * s[:, r:r + 1], axis=0, keepdims=True) + b_ref[0]


def _ada(cv, n_rows, w_ada, b_ada):
    depth, d, n6 = w_ada.shape
    tn = min(512, n6)
    return pl.pallas_call(
        functools.partial(_ada_kernel, n_rows=n_rows),
        out_shape=jax.ShapeDtypeStruct((depth, 8, n6), f32),
        grid=(depth, n6 // tn),
        in_specs=[
            pl.BlockSpec((d, 8), lambda l, j: (0, 0)),
            pl.BlockSpec((1, d, tn), lambda l, j: (l, 0, j)),
            pl.BlockSpec((1, 1, tn), lambda l, j: (l, 0, j)),
        ],
        out_specs=pl.BlockSpec((1, 8, tn), lambda l, j: (l, 0, j)),
        compiler_params=_params("arbitrary", "arbitrary"),
        name="ada",
    )(cv.T, w_ada, b_ada.reshape(depth, 1, n6))


def _norm_kernel(*refs, has_resid, has_norm):
    it = iter(refs)
    h_ref = next(it)
    if has_resid:
        f_ref, gate_ref = next(it), next(it)
    if has_norm:
        g_ref, shift_ref, scale_ref = next(it), next(it), next(it)
    h = h_ref[...]
    if has_resid:
        h = h + gate_ref[0] * f_ref[...]
        next(it)[...] = h
    if has_norm:
        ms = jnp.mean(h * h, axis=-1, keepdims=True)
        y = h * lax.rsqrt(ms + EPS) * g_ref[...]
        n_ref = next(it)
        n_ref[...] = (y * (1.0 + scale_ref[0]) + shift_ref[0]).astype(n_ref.dtype)


def _resid_norm(h, rows, seg_of, mods, resid=None, norm=None, n_dtype=bf16):
    t, d = h.shape
    tr = ROW_TILE
    row_spec = pl.BlockSpec((tr, d), lambda i: (i, 0))

    def mod_spec(layer, k):
        return pl.BlockSpec((1, 1, d), lambda i: ((layer * 8 + seg_of(i, tr)) * 6 + k, 0, 0))

    args, in_specs, out_shape, out_specs, aliases = [h], [row_spec], [], [], {}
    if resid is not None:
        f, layer, k_gate = resid
        args += [f, mods]
        in_specs += [row_spec, mod_spec(layer, k_gate)]
        out_shape.append(jax.ShapeDtypeStruct((t, d), f32))
        out_specs.append(row_spec)
        aliases = {0: 0}
    if norm is not None:
        g, layer, k_shift, k_scale = norm
        args += [g.reshape(1, d), mods, mods]
        in_specs += [pl.BlockSpec((1, d), lambda i: (0, 0)), mod_spec(layer, k_shift), mod_spec(layer, k_scale)]
        out_shape.append(jax.ShapeDtypeStruct((t, d), n_dtype))
        out_specs.append(row_spec)
    outs = pl.pallas_call(
        functools.partial(_norm_kernel, has_resid=resid is not None, has_norm=norm is not None),
        out_shape=out_shape,
        grid=(rows // tr,),
        in_specs=in_specs,
        out_specs=out_specs,
        input_output_aliases=aliases,
        compiler_params=_params("arbitrary"),
        name="resid_norm",
    )(*args)
    return outs


def _mm_kernel(a_ref, b_ref, o_ref):
    o_ref[...] = jnp.dot(a_ref[...], b_ref[...], preferred_element_type=f32).astype(o_ref.dtype)


def _matmul(a, b, layer, out_dtype, n=None, tm=512, tn=1024):
    m, k = a.shape
    n = n or b.shape[2]
    tm, tn = min(tm, m), math.gcd(tn, n)
    assert m % tm == 0 and tn % 128 == 0
    return pl.pallas_call(
        _mm_kernel,
        out_shape=jax.ShapeDtypeStruct((m, n), out_dtype),
        grid=(n // tn, m // tm),
        in_specs=[pl.BlockSpec((tm, k), lambda j, i: (i, 0)), pl.BlockSpec((None, k, tn), lambda j, i: (layer, 0, j))],
        out_specs=pl.BlockSpec((tm, tn), lambda j, i: (i, j)),
        compiler_params=_params("arbitrary", "arbitrary"),
        name="in_proj",
    )(a, b)


def _gates_kernel(n_ref, w_ref, b_ref, g_ref, c_ref, *, mh):
    tr = n_ref.shape[0]
    mg = jnp.dot(n_ref[...], w_ref[...], preferred_element_type=f32) + b_ref[...]
    g = GATE_CAP * jnp.tanh(mg * (1.0 / GATE_CAP))
    lane = lax.broadcasted_iota(jnp.int32, g.shape, 1)
    is_forget = (lane // mh) % 2 == 1
    log_sig = jnp.minimum(g, 0.0) - jnp.log(1.0 + jnp.exp(-jnp.abs(g)))
    val = jnp.where(is_forget, log_sig, g)
    g_ref[...] = val
    r = lax.broadcasted_iota(jnp.int32, (M_CHUNK, M_CHUNK), 0)
    c = lax.broadcasted_iota(jnp.int32, (M_CHUNK, M_CHUNK), 1)
    lower = (c <= r).astype(f32)
    upper = (c >= r).astype(f32)
    fwd = lax.broadcasted_iota(jnp.int32, (M_CHUNK, g.shape[1]), 1) < 2 * mh
    for ch in range(tr // M_CHUNK):
        v = val[ch * M_CHUNK:(ch + 1) * M_CHUNK]
        cf = jnp.dot(lower, v, preferred_element_type=f32, precision=lax.Precision.HIGHEST)
        cb = jnp.dot(upper, v, preferred_element_type=f32, precision=lax.Precision.HIGHEST)
        c_ref[ch * M_CHUNK:(ch + 1) * M_CHUNK, :] = jnp.where(fwd, cf, cb)


def _gates(n, w_g, layer, gate_b, mh):
    t, d = n.shape
    ng = 4 * mh
    tr = ROW_TILE
    spec = pl.BlockSpec((tr, ng), lambda i: (i, 0))
    return pl.pallas_call(
        functools.partial(_gates_kernel, mh=mh),
        out_shape=[jax.ShapeDtypeStruct((t, ng), f32)] * 2,
        grid=(t // tr,),
        in_specs=[pl.BlockSpec((tr, d), lambda i: (i, 0)), pl.BlockSpec((None, d, ng), lambda i: (layer, 0, 0)),
                  pl.BlockSpec((1, ng), lambda i: (0, 0))],
        out_specs=[spec, spec],
        compiler_params=_params("arbitrary"),
        name="gates",
    )(n, w_g, gate_b.reshape(1, ng).astype(f32))


def _qk_kernel(x_ref, g_ref, cos_ref, sin_ref, o_ref, *, hd):
    x = x_ref[...].astype(f32)
    cos, sin = cos_ref[...], sin_ref[...]
    r = lax.broadcasted_iota(jnp.int32, (hd, hd), 0)
    c = lax.broadcasted_iota(jnp.int32, (hd, hd), 1)
    swap = jnp.where(r == (c ^ (hd // 4)), 1.0, 0.0).astype(bf16)
    ones = jnp.ones((hd, hd), bf16)
    for s in range(x.shape[1] // hd):
        sl = slice(s * hd, (s + 1) * hd)
        xs = x[:, sl]
        ssq = jnp.dot((xs * xs).astype(bf16), ones, preferred_element_type=f32)
        y = xs * lax.rsqrt(ssq * (1.0 / hd) + EPS) * g_ref[:, sl]
        partner = jnp.dot(y.astype(bf16), swap, preferred_element_type=f32)
        o_ref[:, sl] = (y * cos + partner * sin).astype(o_ref.dtype)


def _qk_prep(parts, gain, cos_t, sin_t, width, n_x_tiles, tiles_per_seq, hd):
    t = parts.shape[0]
    tr = ROW_TILE
    wq = min(1024, width)

    def tbl(i, j):
        return (jnp.where(i < n_x_tiles, i % tiles_per_seq, tiles_per_seq), 0)

    return pl.pallas_call(
        functools.partial(_qk_kernel, hd=hd),
        out_shape=jax.ShapeDtypeStruct((t, width), bf16),
        grid=(t // tr, width // wq),
        in_specs=[pl.BlockSpec((tr, wq), lambda i, j: (i, j)), pl.BlockSpec((1, wq), lambda i, j: (0, j)),
                  pl.BlockSpec((tr, hd), tbl), pl.BlockSpec((tr, hd), tbl)],
        out_specs=pl.BlockSpec((tr, wq), lambda i, j: (i, j)),
        compiler_params=_params("arbitrary", "arbitrary"),
        name="qk_prep",
    )(parts, gain, cos_t, sin_t)


def _attn_kernel(q_ref, kx_ref, kc_ref, vx_ref, vc_ref, lam_ref, bound_ref, g_ref, o_ref, k_ref, vt_ref, *, hd, tkv,
                 tt, nq_x, with_ctx_queries, score_bound):
    i = pl.program_id(2)
    tq = q_ref.shape[0]
    l_ctx, s = kc_ref.shape[0], kx_ref.shape[0]

    @pl.when(i == 0)
    def _():
        k_ref[0:l_ctx] = kc_ref[...]
        vt_ref[:, 0:l_ctx] = vc_ref[...].T
        for c in range(s // tt):
            k_ref[l_ctx + c * tt:l_ctx + (c + 1) * tt] = kx_ref[c * tt:(c + 1) * tt]
            vt_ref[:, l_ctx + c * tt:l_ctx + (c + 1) * tt] = vx_ref[c * tt:(c + 1) * tt].T

    def attend(n_keys, tk):
        qt = q_ref[...].T
        qt0, qt1 = qt[:hd], qt[hd:]

        def scores(c):
            k = k_ref[c * tk:(c + 1) * tk]
            return jnp.concatenate([jnp.dot(k[:, :hd], qt0, preferred_element_type=f32),
                                    jnp.dot(k[:, hd:], qt1, preferred_element_type=f32)], axis=1)

        m = l_sum = acc = None
        st = scores(0)
        for c in range(n_keys // tk):
            st_next = scores(c + 1) if (c + 1) * tk < n_keys else None
            if score_bound:
                m_new = bound_ref[...]
            else:
                m_c = jnp.max(st, axis=0, keepdims=True)
                m_new = m_c if m is None else jnp.maximum(m, m_c)
            p = jnp.exp(st - m_new)
            pv = jnp.dot(vt_ref[:, c * tk:(c + 1) * tk], p.astype(bf16), preferred_element_type=f32)
            if m is None:
                l_sum, acc = jnp.sum(p, axis=0, keepdims=True), pv
            elif score_bound:
                l_sum, acc = l_sum + jnp.sum(p, axis=0, keepdims=True), acc + pv
            else:
                alpha = jnp.exp(m - m_new)
                l_sum, acc = alpha * l_sum + jnp.sum(p, axis=0, keepdims=True), alpha * acc + pv
            m, st = m_new, st_next
        acc = acc / l_sum
        a = (acc[:, :tq] - lam_ref[...] * acc[:, tq:]).T
        ms = jnp.mean(a * a, axis=-1, keepdims=True)
        o_ref[...] = (a * lax.rsqrt(ms + EPS) * g_ref[...]).astype(o_ref.dtype)

    if with_ctx_queries:
        pl.when(i < nq_x)(lambda: attend(l_ctx + s, tkv))
        pl.when(i >= nq_x)(lambda: attend(l_ctx, l_ctx))
    else:
        attend(l_ctx + s, tkv)


def _attention(qk2, parts, lam, bound, out_gain, dims, with_ctx_queries):
    b, s, l, heads, hd = dims
    a_w = heads * 2 * hd
    t = qk2.shape[0]
    tq = ROW_TILE
    n_keys = s + l
    tkv = next(n_keys // n for n in range(1, n_keys) if n_keys % (n * 128) == 0 and n_keys // n <= ATTN_MAX_CHUNK)
    nq_x, nq_c = s // tq, l // tq
    nq = nq_x + (nq_c if with_ctx_queries else 0)
    hw = 2 * hd
    k_col, v_col = a_w // hw, 2 * a_w // hw
    ctx_row = b * s // l

    def q_idx(bi, h, i):
        return (jnp.where(i < nq_x, bi * nq_x + i, b * nq_x + bi * nq_c + (i - nq_x)), h)

    def call(score_bound):
        return pl.pallas_call(
            functools.partial(_attn_kernel, hd=hd, tkv=tkv, tt=min(512, s), nq_x=nq_x,
                              with_ctx_queries=with_ctx_queries, score_bound=score_bound),
            out_shape=jax.ShapeDtypeStruct((t if with_ctx_queries else b * s, a_w), bf16),
            grid=(b, heads, nq),
            in_specs=[
                pl.BlockSpec((tq, hw), q_idx),
                pl.BlockSpec((s, hw), lambda bi, h, i: (bi, k_col + h)),
                pl.BlockSpec((l, hw), lambda bi, h, i: (ctx_row + bi, k_col + h)),
                pl.BlockSpec((s, hw), lambda bi, h, i: (bi, v_col + h)),
                pl.BlockSpec((l, hw), lambda bi, h, i: (ctx_row + bi, v_col + h)),
                pl.BlockSpec((1, 1), lambda bi, h, i: (0, 0)),
                pl.BlockSpec((1, 1), lambda bi, h, i: (0, 0)),
                pl.BlockSpec((1, hw), lambda bi, h, i: (0, 0)),
            ],
            out_specs=pl.BlockSpec((tq, hw), q_idx),
            scratch_shapes=[pltpu.VMEM((n_keys, hw), bf16), pltpu.VMEM((hw, n_keys), bf16)],
            compiler_params=_params("arbitrary", "arbitrary", "arbitrary"),
            name="diff_attn",
        )(qk2, qk2, qk2, parts, parts, lam, bound, out_gain)

    return lax.cond(bound[0, 0] <= ATTN_MAX_BOUND, lambda: call(True), lambda: call(False))


def _mprep_kernel(x_ref, p_ref, n_ref, w_ref, b_ref, sc_ref, o_ref, *, s, l, bs):
    tr = x_ref.shape[0]
    row0 = pl.program_id(0) * tr
    in_x = row0 < bs
    rel = jnp.where(in_x, row0 % s, (row0 - bs) % l)
    seq_len = jnp.where(in_x, s, l)
    keep_prev = jnp.where(rel == 0, 0.0, 1.0)
    keep_next = jnp.where(rel + tr == seq_len, 0.0, 1.0)
    x = x_ref[...].astype(f32)
    prev = p_ref[...].astype(f32)[BF16_ROWS - 1:BF16_ROWS] * keep_prev
    nxt = n_ref[...].astype(f32)[0:1] * keep_next
    rows = lax.broadcasted_iota(jnp.int32, x.shape, 0)
    xm = jnp.where(rows == 0, prev, pltpu.roll(x, 1, 0))
    xp = jnp.where(rows == tr - 1, nxt, pltpu.roll(x, tr - 1, 0))
    w = w_ref[...]
    u = b_ref[...] + xm * w[0:1] + x * w[1:2] + xp * w[2:3]
    o_ref[...] = (u * _sigmoid(u) * sc_ref[...]).astype(o_ref.dtype)


def _mlstm_prep(parts, col0, conv_w, conv_b, post, dims):
    b, s, l = dims
    t = parts.shape[0]
    width = conv_w.shape[1]
    tr = ROW_TILE
    wc = math.gcd(1024, math.gcd(width, col0))
    cb = col0 // wc
    sub = tr // BF16_ROWS
    last = t // BF16_ROWS - 1
    return pl.pallas_call(
        functools.partial(_mprep_kernel, s=s, l=l, bs=b * s),
        out_shape=jax.ShapeDtypeStruct((t, width), bf16),
        grid=(t // tr, width // wc),
        in_specs=[
            pl.BlockSpec((tr, wc), lambda i, j: (i, cb + j)),
            pl.BlockSpec((BF16_ROWS, wc), lambda i, j: (jnp.maximum(i * sub - 1, 0), cb + j)),
            pl.BlockSpec((BF16_ROWS, wc), lambda i, j: (jnp.minimum((i + 1) * sub, last), cb + j)),
            pl.BlockSpec((M_CONV, wc), lambda i, j: (0, j)),
            pl.BlockSpec((1, wc), lambda i, j: (0, j)),
            pl.BlockSpec((1, wc), lambda i, j: (0, j)),
        ],
        out_specs=pl.BlockSpec((tr, wc), lambda i, j: (i, j)),
        compiler_params=_params("arbitrary", "arbitrary"),
        name="mlstm_prep",
    )(parts, parts, parts, conv_w, conv_b, post)


def _scan_chunk(d, q, k, v, li_col, bs_col, li_row, bs_row, c_st, n_st, m_st):
    r = lax.broadcasted_iota(jnp.int32, (M_CHUNK, M_CHUNK), 0)
    c = lax.broadcasted_iota(jnp.int32, (M_CHUNK, M_CHUNK), 1)
    visible = (c <= r) if d == 0 else (c >= r)
    last = M_CHUNK - 1 if d == 0 else 0
    b_end = bs_row[:, last:last + 1]

    d_log = jnp.where(visible, bs_col + (li_row - bs_row), NEG)
    inter = bs_col + m_st
    m_pos = jnp.maximum(jnp.max(d_log, axis=1, keepdims=True), inter)
    w_intra = jnp.exp(d_log - m_pos)
    w_inter = jnp.exp(inter - m_pos)
    sc = lax.dot_general(q, k, (((1,), (1,)), ((), ())), preferred_element_type=f32) * w_intra
    num = jnp.dot(sc.astype(bf16), v, preferred_element_type=f32) + w_inter * jnp.dot(
        q, c_st.astype(bf16), preferred_element_type=f32)
    qn = jnp.sum(q.astype(f32) * n_st, axis=1, keepdims=True)
    den = jnp.sum(sc, axis=1, keepdims=True) + w_inter * qn
    hidden = num / jnp.maximum(jnp.abs(den), jnp.exp(-m_pos))

    w_log = b_end - bs_col + li_col
    m_new = jnp.maximum(b_end + m_st, jnp.max(w_log, axis=0, keepdims=True))
    decay = jnp.exp(b_end + m_st - m_new)
    wk = jnp.exp(w_log - m_new) * k.astype(f32)
    c_new = decay * c_st + lax.dot_general(wk.astype(bf16), v, (((0,), (0,)), ((), ())),
                                           preferred_element_type=f32)
    n_new = decay * n_st + jnp.sum(wk, axis=0, keepdims=True)
    return hidden, c_new, n_new, m_new


def _scan_kernel(*refs, mh, n_sub, n_heads):
    ins, outs, (c_ref, n_ref, m_ref) = refs[:10], refs[10:12], refs[12:]
    ng = 4 * mh
    dk, dv = c_ref.shape[1:]

    @pl.when(pl.program_id(2) == 0)
    def _():
        for ref in (c_ref, n_ref, m_ref):
            ref[...] = jnp.zeros(ref.shape, f32)

    chains = []
    for ci in range(2 * n_heads):
        hh, d = divmod(ci, 2)
        q_ref, k_ref, v_ref, gcol_ref, grow_ref = ins[5 * d:5 * d + 5]
        col_i = d * 2 * mh + pl.program_id(1) * n_heads + hh
        col_f = ng + col_i + mh
        gcol = gcol_ref[...]
        lane = lax.broadcasted_iota(jnp.int32, gcol.shape, 1)
        chains.append(dict(
            d=d, hh=hh, q=q_ref, k=k_ref, v=v_ref,
            li_col=jnp.sum(jnp.where(lane == col_i, gcol, 0.0), axis=1, keepdims=True),
            bs_col=jnp.sum(jnp.where(lane == col_f, gcol, 0.0), axis=1, keepdims=True),
            li_row=grow_ref[pl.ds(col_i, 1), :], bs_row=grow_ref[pl.ds(col_f, 1), :],
            state=(c_ref[ci], n_ref[ci], m_ref[ci]), hidden=[None] * n_sub))
    for j in range(n_sub):
        for ch in chains:
            sub = j if ch["d"] == 0 else n_sub - 1 - j
            rows = slice(sub * M_CHUNK, (sub + 1) * M_CHUNK)
            qk_cols = slice(ch["hh"] * dk, (ch["hh"] + 1) * dk)
            v_cols = slice(ch["hh"] * dv, (ch["hh"] + 1) * dv)
            ch["hidden"][sub], *ch["state"] = _scan_chunk(
                ch["d"], ch["q"][rows, qk_cols], ch["k"][rows, qk_cols], ch["v"][rows, v_cols],
                ch["li_col"][rows], ch["bs_col"][rows], ch["li_row"][:, rows], ch["bs_row"][:, rows], *ch["state"])
    for ch in chains:
        outs[ch["d"]][:, ch["hh"] * dv:(ch["hh"] + 1) * dv] = jnp.concatenate(ch["hidden"], axis=0).astype(bf16)
    for idx, ref in enumerate((c_ref, n_ref, m_ref)):
        ref[...] = jnp.stack([ch["state"][idx] for ch in chains])


def _mlstm_scan(mqk, parts, v_col0, gc_col, gc_row, dims):
    b, s, l, mh, dk, dv = dims
    t = mqk.shape[0]
    rows = SCAN_SUB * M_CHUNK
    assert l % rows == 0 and s % rows == 0
    n_lc, n_sc = l // rows, s // rows
    ctx0 = b * s // rows
    ng2 = gc_col.shape[1]

    def chunk(d, bi, st):
        in_ctx = st < n_lc
        local = jnp.where(in_ctx, st, st - n_lc)
        n_local = jnp.where(in_ctx, n_lc, n_sc)
        pos = local if d == 0 else n_local - 1 - local
        return jnp.where(in_ctx, ctx0 + bi * n_lc, bi * n_sc) + pos

    nh = math.gcd(SCAN_HEADS, mh)
    if v_col0 % (nh * dv):
        nh = 1
    vb = v_col0 // (nh * dv)

    def dir_specs(d):
        return [
            pl.BlockSpec((rows, nh * dk), lambda bi, g, st: (chunk(d, bi, st), g)),
            pl.BlockSpec((rows, nh * dk), lambda bi, g, st: (chunk(d, bi, st), mh // nh + g)),
            pl.BlockSpec((rows, nh * dv), lambda bi, g, st: (chunk(d, bi, st), vb + g)),
            pl.BlockSpec((rows, ng2), lambda bi, g, st: (chunk(d, bi, st), 0)),
            pl.BlockSpec((ng2, rows), lambda bi, g, st: (0, chunk(d, bi, st))),
        ]

    def out_spec(d):
        return pl.BlockSpec((rows, nh * dv), lambda bi, g, st: (chunk(d, bi, st), g))

    state = [pltpu.VMEM((2 * nh, dk, dv), f32), pltpu.VMEM((2 * nh, 1, dk), f32), pltpu.VMEM((2 * nh, 1, 1), f32)]
    args = (mqk, mqk, parts, gc_col, gc_row)
    return pl.pallas_call(
        functools.partial(_scan_kernel, mh=mh, n_sub=SCAN_SUB, n_heads=nh),
        out_shape=[jax.ShapeDtypeStruct((t, mh * dv), bf16)] * 2,
        grid=(b, mh // nh, n_lc + n_sc),
        in_specs=dir_specs(0) + dir_specs(1),
        out_specs=[out_spec(0), out_spec(1)],
        scratch_shapes=state,
        compiler_params=_params("arbitrary", "arbitrary", "arbitrary"),
        name="mlstm_scan",
    )(*args, *args)


def _mout_kernel(hf_ref, hb_ref, o_ref, g_ref, out_ref):
    hs = hf_ref[...].astype(f32) + hb_ref[...].astype(f32)
    ms = jnp.mean(hs * hs, axis=-1, keepdims=True)
    y = hs * lax.rsqrt(ms + EPS) * g_ref[0]
    out_ref[...] = (y * _sigmoid(o_ref[...].astype(f32))).astype(out_ref.dtype)


def _mlstm_out(h_fwd, h_bwd, parts, o_col0, head_g, rows):
    m_w = h_fwd.shape[1]
    mh, dv = head_g.shape
    tr = ROW_TILE
    ob = o_col0 // dv
    head_spec = pl.BlockSpec((tr, dv), lambda i, h: (i, h))
    return pl.pallas_call(
        _mout_kernel,
        out_shape=jax.ShapeDtypeStruct((rows, m_w), bf16),
        grid=(rows // tr, mh),
        in_specs=[head_spec, head_spec, pl.BlockSpec((tr, dv), lambda i, h: (i, ob + h)),
                  pl.BlockSpec((1, 1, dv), lambda i, h: (h, 0, 0))],
        out_specs=head_spec,
        compiler_params=_params("arbitrary", "arbitrary"),
        name="mlstm_out",
    )(h_fwd, h_bwd, parts, head_g.reshape(mh, 1, dv).astype(f32))


def _merge_kernel(a_ref, wa_ref, m_ref, wm_ref, ga_ref, gb_ref, o_ref):
    ya = jnp.dot(a_ref[...], wa_ref[...], preferred_element_type=f32)
    ym = jnp.dot(m_ref[...], wm_ref[...], preferred_element_type=f32)
    y = _sigmoid(ga_ref[...].astype(f32)) * ya + _sigmoid(gb_ref[...].astype(f32)) * ym
    o_ref[...] = y.astype(o_ref.dtype)


def _merge(a_flat, m_flat, w_a, w_m, layer, parts, ga_col0, rows):
    d = w_a.shape[2]
    tm, tn = 512, min(512, d)
    ga, gb = ga_col0 // tn, (ga_col0 + d) // tn
    return pl.pallas_call(
        _merge_kernel,
        out_shape=jax.ShapeDtypeStruct((rows, d), bf16),
        grid=(d // tn, rows // tm),
        in_specs=[
            pl.BlockSpec((tm, a_flat.shape[1]), lambda j, i: (i, 0)),
            pl.BlockSpec((None, w_a.shape[1], tn), lambda j, i: (layer, 0, j)),
            pl.BlockSpec((tm, m_flat.shape[1]), lambda j, i: (i, 0)),
            pl.BlockSpec((None, w_m.shape[1], tn), lambda j, i: (layer, 0, j)),
            pl.BlockSpec((tm, tn), lambda j, i: (i, ga + j)),
            pl.BlockSpec((tm, tn), lambda j, i: (i, gb + j)),
        ],
        out_specs=pl.BlockSpec((tm, tn), lambda j, i: (i, j)),
        compiler_params=_params("arbitrary", "arbitrary"),
        name="branch_merge",
    )(a_flat, w_a, m_flat, w_m, parts, parts)


def _outproj_kernel(y_ref, w_ref, h_ref, gate_ref, o_ref):
    o_ref[...] = h_ref[...] + gate_ref[0] * jnp.dot(y_ref[...], w_ref[...], preferred_element_type=f32)


def _out_proj(y, w_out, h, mods, layer, seg_of, rows):
    t, d = h.shape
    tm, tn = 512, min(512, d)
    return pl.pallas_call(
        _outproj_kernel,
        out_shape=jax.ShapeDtypeStruct((t, d), f32),
        grid=(d // tn, rows // tm),
        in_specs=[
            pl.BlockSpec((tm, d), lambda j, i: (i, 0)),
            pl.BlockSpec((None, d, tn), lambda j, i: (layer, 0, j)),
            pl.BlockSpec((tm, tn), lambda j, i: (i, j)),
            pl.BlockSpec((1, 1, tn), lambda j, i: ((layer * 8 + seg_of(i, tm)) * 6 + 2, 0, j)),
        ],
        out_specs=pl.BlockSpec((tm, tn), lambda j, i: (i, j)),
        input_output_aliases={2: 0},
        compiler_params=_params("arbitrary", "arbitrary"),
        name="out_proj",
    )(y, w_out, h, mods)


def _route_kernel(h_ref, g_ref, shift_ref, scale_ref, wr_ref, bias_ref, n_ref, r_ref):
    h = h_ref[...]
    ms = jnp.mean(h * h, axis=-1, keepdims=True)
    n = h * lax.rsqrt(ms + EPS) * g_ref[...] * (1.0 + scale_ref[0]) + shift_ref[0]
    n_ref[...] = n
    logits = lax.dot_general(wr_ref[...], n, (((1,), (1,)), ((), ())), preferred_element_type=f32,
                             precision=lax.Precision.HIGHEST)
    score = _sigmoid(logits)
    sel = score + bias_ref[...]
    sel_e = [sel[e:e + 1] for e in range(N_GROUPS * GROUP_SIZE)]
    score_e = [score[e:e + 1] for e in range(N_GROUPS * GROUP_SIZE)]

    def top2_sum(v):
        best = v[0] + v[1]
        for a in range(GROUP_SIZE):
            for b in range(a + 1, GROUP_SIZE):
                if (a, b) != (0, 1):
                    best = jnp.maximum(best, v[a] + v[b])
        return best

    g_best = top2_sum(sel_e[:GROUP_SIZE])
    grp = jnp.zeros(g_best.shape, jnp.int32)
    for g in range(1, N_GROUPS):
        cand = top2_sum(sel_e[g * GROUP_SIZE:(g + 1) * GROUP_SIZE])
        better = cand > g_best
        grp = jnp.where(better, g, grp)
        g_best = jnp.where(better, cand, g_best)

    def in_group(rows):
        out = []
        for j in range(GROUP_SIZE):
            v = rows[j]
            for g in range(1, N_GROUPS):
                v = jnp.where(grp == g, rows[g * GROUP_SIZE + j], v)
            out.append(v)
        return out

    v, sc = in_group(sel_e), in_group(score_e)
    j1, b1, c1 = jnp.zeros(grp.shape, jnp.int32), v[0], sc[0]
    for j in range(1, GROUP_SIZE):
        better = v[j] > b1
        j1, b1, c1 = jnp.where(better, j, j1), jnp.where(better, v[j], b1), jnp.where(better, sc[j], c1)
    j2 = jnp.full(grp.shape, -1, jnp.int32)
    b2, c2 = jnp.full(b1.shape, -jnp.inf, f32), jnp.zeros(b1.shape, f32)
    for j in range(GROUP_SIZE):
        better = (j1 != j) & ((v[j] > b2) | (j2 < 0))
        j2, b2, c2 = jnp.where(better, j, j2), jnp.where(better, v[j], b2), jnp.where(better, sc[j], c2)
    tot = c1 + c2
    w1, w2 = c1 / tot, c2 / tot
    first_low = j1 < j2
    lo, hi = jnp.minimum(j1, j2), jnp.maximum(j1, j2)
    pair_in_group = jnp.zeros(lo.shape, jnp.int32)
    for idx, (p_lo, p_hi) in enumerate(PAIR_ORDER):
        pair_in_group = jnp.where((lo == p_lo) & (hi == p_hi), idx, pair_in_group)
    r_ref[...] = jnp.zeros(r_ref.shape, f32)
    r_ref[0:1, :] = (grp * 6 + pair_in_group).astype(f32)
    r_ref[1:2, :] = jnp.where(first_low, w1, w2)
    r_ref[2:3, :] = jnp.where(first_low, w2, w1)


def _norm_route(h, g, mods, layer, seg_of, w_router_t, bias, rows):
    t, d = h.shape
    e = w_router_t.shape[0]
    tr = ROW_TILE

    def mod_spec(k):
        return pl.BlockSpec((1, 1, d), lambda i: ((layer * 8 + seg_of(i, tr)) * 6 + k, 0, 0))

    return pl.pallas_call(
        _route_kernel,
        out_shape=[jax.ShapeDtypeStruct((rows, d), f32), jax.ShapeDtypeStruct((8, rows), f32)],
        grid=(rows // tr,),
        in_specs=[pl.BlockSpec((tr, d), lambda i: (i, 0)), pl.BlockSpec((1, d), lambda i: (0, 0)),
                  mod_spec(3), mod_spec(4), pl.BlockSpec((e, d), lambda i: (0, 0)),
                  pl.BlockSpec((e, 1), lambda i: (0, 0))],
        out_specs=[pl.BlockSpec((tr, d), lambda i: (i, 0)), pl.BlockSpec((8, tr), lambda i: (0, i))],
        compiler_params=_params("arbitrary"),
        name="norm_route",
    )(h, g.reshape(1, d), mods, mods, w_router_t, bias.reshape(e, 1).astype(f32))


def _moe_kernel(lo_ref, hi_ref, nv_ref, tok_ref, x_hbm, w_ref, wg_lo, wu_lo, wd_lo, wg_hi, wu_hi, wd_hi,
                out_hbm, xbuf, ybuf, sem_in, sem_out):
    blk = pl.program_id(0)
    slot = blk % 2

    def row_in(b, r):
        return pltpu.make_async_copy(x_hbm.at[pl.ds(tok_ref[b * MOE_BLOCK + r], 1)],
                                     xbuf.at[b % 2, pl.ds(r, 1)], sem_in.at[b % 2])

    def row_out(b, r):
        return pltpu.make_async_copy(ybuf.at[b % 2, pl.ds(r, 1)],
                                     out_hbm.at[pl.ds(tok_ref[b * MOE_BLOCK + r], 1)], sem_out.at[b % 2])

    def for_rows(fn):
        def body(r, c):
            fn(r)
            return c

        lax.fori_loop(0, MOE_BLOCK, body, 0, unroll=8)

    def start_out(b):
        for_rows(lambda r: pl.when(r < nv_ref[b])(lambda: row_out(b, r).start()))

    def wait_in(b):
        pltpu.make_async_copy(x_hbm.at[pl.ds(0, MOE_BLOCK)], xbuf.at[b % 2], sem_in.at[b % 2]).wait()

    def wait_out(b):
        nv = nv_ref[b]
        p = 1
        while p <= MOE_BLOCK:
            rows = pl.ds(0, p)
            pl.when((nv & p) != 0)(
                pltpu.make_async_copy(ybuf.at[b % 2, rows], out_hbm.at[rows], sem_out.at[b % 2]).wait)
            p *= 2

    def has_rows(b):
        n_blk = pl.num_programs(0)
        return jnp.logical_and(b < n_blk, nv_ref[jnp.minimum(b, n_blk - 1)] > 0)

    @pl.when(nv_ref[blk] > 0)
    def _():
        @pl.when(blk == 0)
        def _():
            for_rows(lambda r: row_in(blk, r).start())

        wait_in(blk)

        @pl.when(has_rows(blk + 1))
        def _():
            for_rows(lambda r: row_in(blk + 1, r).start())

        x = xbuf[slot].astype(bf16)
        eye = (lax.broadcasted_iota(jnp.int32, (MOE_BLOCK, MOE_BLOCK), 0)
               == lax.broadcasted_iota(jnp.int32, (MOE_BLOCK, MOE_BLOCK), 1))
        w = w_ref[0]

        def expert(wg, wu, wd, w_row):
            a = jnp.dot(x, wg[0], preferred_element_type=f32)
            u = jnp.dot(x, wu[0], preferred_element_type=f32)
            y = jnp.dot((a * _sigmoid(a) * u).astype(bf16), wd[0], preferred_element_type=f32)
            return y * jnp.sum(jnp.where(eye, w_row, 0.0), axis=1, keepdims=True)

        ybuf[slot] = expert(wg_lo, wu_lo, wd_lo, w[0:1]) + expert(wg_hi, wu_hi, wd_hi, w[1:2])

        @pl.when(blk > 0)
        def _():
            wait_out(blk - 1)

        start_out(blk)

        @pl.when(jnp.logical_not(has_rows(blk + 1)))
        def _():
            wait_out(blk)


def _moe(n2, route, w_gate, w_up, w_down, layer, n_tok):
    d = n2.shape[1]
    dff = w_gate.shape[-1]
    blk = MOE_BLOCK
    n_blk = -(-n_tok // blk) + N_PAIRS
    n_pad = n_blk * blk
    pair = route[0, :n_tok].astype(jnp.int32)
    onehot = (pair[:, None] == jnp.arange(N_PAIRS, dtype=jnp.int32)[None, :]).astype(jnp.int32)
    csum = jnp.cumsum(onehot, axis=0)
    rank = jnp.sum(csum * onehot, axis=1) - 1
    counts = csum[-1]
    padded = (counts + blk - 1) // blk * blk
    pad_end = jnp.cumsum(padded)
    pad_start = pad_end - padded
    dest = pad_start[pair] + rank
    buf_tok = jnp.zeros((n_pad,), jnp.int32).at[dest].set(jnp.arange(n_tok, dtype=jnp.int32))
    wts = jnp.zeros((2, n_pad), f32).at[:, dest].set(route[1:3, :n_tok])
    wts = wts.reshape(2, n_blk, blk).transpose(1, 0, 2)
    blk_start = jnp.arange(n_blk, dtype=jnp.int32) * blk
    seg = jnp.minimum(jnp.searchsorted(pad_end, blk_start, side="right"), N_PAIRS - 1).astype(jnp.int32)
    nv = jnp.clip(counts[seg] - (blk_start - pad_start[seg]), 0, blk).astype(jnp.int32)
    pair_tbl = jnp.asarray(PAIR_ORDER, jnp.int32)
    e_lo = (seg // len(PAIR_ORDER)) * GROUP_SIZE + pair_tbl[seg % len(PAIR_ORDER), 0]
    e_hi = (seg // len(PAIR_ORDER)) * GROUP_SIZE + pair_tbl[seg % len(PAIR_ORDER), 1]

    def w_in(sel):
        return pl.BlockSpec((None, 1, d, dff), lambda i, lo, hi, nvr, tok: (layer, (lo, hi)[sel][i], 0, 0),
                            pipeline_mode=pl.Buffered(1))

    def w_out(sel):
        return pl.BlockSpec((None, 1, dff, d), lambda i, lo, hi, nvr, tok: (layer, (lo, hi)[sel][i], 0, 0),
                            pipeline_mode=pl.Buffered(1))

    return pl.pallas_call(
        _moe_kernel,
        out_shape=jax.ShapeDtypeStruct((n_tok, d), f32),
        grid_spec=pltpu.PrefetchScalarGridSpec(
            num_scalar_prefetch=4,
            grid=(n_blk,),
            in_specs=[pl.BlockSpec(memory_space=pl.ANY),
                      pl.BlockSpec((1, 2, blk), lambda i, lo, hi, nvr, tok: (i, 0, 0)),
                      w_in(0), w_in(0), w_out(0), w_in(1), w_in(1), w_out(1)],
            out_specs=pl.BlockSpec(memory_space=pl.ANY),
            scratch_shapes=[pltpu.VMEM((2, blk, d), f32), pltpu.VMEM((2, blk, d), f32),
                            pltpu.SemaphoreType.DMA((2,)), pltpu.SemaphoreType.DMA((2,))],
        ),
        compiler_params=_params("arbitrary"),
        name="moe",
    )(e_lo, e_hi, nv, buf_tok, n2, wts, w_gate, w_up, w_down, w_gate, w_up, w_down)


def _rope_tables(n_rows, hd, pad_rows):
    n_freq = hd // 4
    inv_freq = ROPE_THETA ** (-jnp.arange(n_freq, dtype=f32) / n_freq)
    pos = jnp.arange(n_rows * GRID_W)
    ang_row = (pos // GRID_W).astype(f32)[:, None] * inv_freq
    ang_col = (pos % GRID_W).astype(f32)[:, None] * inv_freq
    ang = jnp.concatenate([ang_row, ang_row, ang_col, ang_col], axis=1)
    sign = jnp.tile(jnp.concatenate([-jnp.ones((n_freq,), f32), jnp.ones((n_freq,), f32)]), 2)
    cos_t = jnp.concatenate([jnp.cos(ang), jnp.ones((pad_rows, hd), f32)], axis=0)
    sin_t = jnp.concatenate([jnp.sin(ang) * sign, jnp.zeros((pad_rows, hd), f32)], axis=0)
    return cos_t, sin_t


def kernel(x, c, ctx, c_ctx, w_ada, b_ada, g_norm1, g_norm2, w_in, q_norm_g, k_norm_g, diff_lambda, attn_head_g,
           conv_w, conv_b, mlstm_gate_b, mlstm_head_g, w_br_attn, w_br_mlstm, w_out, w_router, router_bias,
           w_gate, w_up, w_down):
    b, s, d = x.shape
    l = ctx.shape[1]
    depth = w_ada.shape[0]
    hd = q_norm_g.shape[-1]
    a_w = w_br_attn.shape[1]
    heads = a_w // (2 * hd)
    mh, dv = mlstm_head_g.shape[-2:]
    qk_w = conv_w.shape[-1] // 2
    dk = qk_w // mh
    m_w = mh * dv
    ng = 4 * mh
    assert w_router.shape[1] == N_GROUPS * GROUP_SIZE
    assert s % ROW_TILE == 0 and l % ROW_TILE == 0 and s % GRID_W == 0 and b + 1 <= 8
    t_x, t_all = b * s, b * s + b * l
    gate_col = 3 * a_w + 2 * qk_w + 2 * m_w
    mq_col, mv_col, mo_col = 3 * a_w, 3 * a_w + 2 * qk_w, 3 * a_w + 2 * qk_w + m_w

    def seg_of(i, tile):
        return jnp.where(i < t_x // tile, i // (s // tile), b)

    w_in_bf = w_in.astype(bf16)
    w_g = w_in_bf[:, :, gate_col:gate_col + ng]
    w_bg = w_in_bf[:, :, gate_col + ng:]
    w_a, w_m, w_o = w_br_attn.astype(bf16), w_br_mlstm.astype(bf16), w_out.astype(bf16)
    w_eg, w_eu, w_ed = w_gate.astype(bf16), w_up.astype(bf16), w_down.astype(bf16)
    cv = jnp.zeros((8, d), f32).at[:b].set(c).at[b].set(c_ctx)
    mods = _ada(cv, b + 1, w_ada, b_ada).reshape(depth * 8 * 6, 1, d)
    cos_t, sin_t = _rope_tables(s // GRID_W, hd, ROW_TILE)
    h = jnp.concatenate([x.reshape(t_x, d), ctx.reshape(b * l, d)], axis=0)
    f = None
    for layer in range(depth):
        need_ctx = layer < depth - 1
        rows = t_all if need_ctx else t_x
        lam_init = 0.8 - 0.6 * math.exp(-0.3 * layer)
        lv = diff_lambda[layer].astype(f32)
        lam = (jnp.exp(jnp.sum(lv[0] * lv[1])) - jnp.exp(jnp.sum(lv[2] * lv[3])) + lam_init).reshape(1, 1)
        if f is None:
            (n1,) = _resid_norm(h, t_all, seg_of, mods, norm=(g_norm1[layer], layer, 0, 1))
        else:
            h, n1 = _resid_norm(h, t_all, seg_of, mods, resid=(f, layer - 1, 5), norm=(g_norm1[layer], layer, 0, 1))
        parts = _matmul(n1, w_in_bf, layer, bf16, n=gate_col)
        branch_gates = _matmul(n1, w_bg, layer, bf16)
        g_col, c_col = _gates(n1, w_g, layer, mlstm_gate_b[layer], mh)

        qk_gain = jnp.concatenate([jnp.tile(q_norm_g[layer].astype(f32), 2 * heads) * hd ** -0.5,
                                   jnp.tile(k_norm_g[layer].astype(f32), 2 * heads)]).reshape(1, 2 * a_w)
        qk2 = _qk_prep(parts, qk_gain, cos_t, sin_t, 2 * a_w, t_x // ROW_TILE, s // ROW_TILE, hd)
        out_gain = (attn_head_g[layer].astype(f32) * (1.0 - lam_init)).reshape(1, 2 * hd)
        bound = (1.01 * hd ** 0.5 * jnp.max(jnp.abs(q_norm_g[layer])) * jnp.max(jnp.abs(k_norm_g[layer])))
        a_flat = _attention(qk2, parts, lam, bound.astype(f32).reshape(1, 1), out_gain, (b, s, l, heads, hd), need_ctx)

        post = jnp.concatenate([jnp.ones((qk_w,), f32), jnp.full((qk_w,), dk ** -0.5, f32)]).reshape(1, 2 * qk_w)
        mqk = _mlstm_prep(parts, mq_col, conv_w[layer].astype(f32), conv_b[layer].reshape(1, -1).astype(f32), post,
                          (b, s, l))
        gc_col = jnp.concatenate([g_col, c_col], axis=1)
        h_fwd, h_bwd = _mlstm_scan(mqk, parts, mv_col, gc_col, gc_col.T, (b, s, l, mh, dk, dv))
        m_flat = _mlstm_out(h_fwd, h_bwd, parts, mo_col, mlstm_head_g[layer], rows)

        y = _merge(a_flat, m_flat, w_a, w_m, layer, branch_gates, 0, rows)
        h = _out_proj(y, w_o, h, mods, layer, seg_of, rows)
        n2, route = _norm_route(h, g_norm2[layer], mods, layer, seg_of, w_router.T.astype(f32), router_bias, rows)
        f = _moe(n2, route, w_eg, w_eu, w_ed, layer, rows)
    (h,) = _resid_norm(h, t_x, seg_of, mods, resid=(f, depth - 1, 5))
    return h[:t_x].reshape(b, s, d)
```

```python
import functools
import math

import jax
import jax.numpy as jnp
from jax import lax
from jax.experimental import pallas as pl
from jax.experimental.pallas import tpu as pltpu

GRID_W = 64
EPS = 1e-6
ROPE_THETA = 10000.0
M_CHUNK = 128
M_CONV = 3
GATE_CAP = 15.0
N_GROUPS = 4
GROUP_SIZE = 4
PAIR_ORDER = ((0, 1), (0, 2), (1, 2), (1, 3), (0, 3), (2, 3))
N_PAIRS = N_GROUPS * len(PAIR_ORDER)
NEG = -1e30

V7X_VMEM_LIMIT = 56 * 1024 * 1024
ROW_TILE = 256
MOE_BLOCK = 128
BF16_ROWS = 16
SCAN_SUB = 2
SCAN_HEADS = 2
MOUT_HEADS = 2
ATTN_MAX_CHUNK = 3072
ATTN_MAX_BOUND = 40.0

f32 = jnp.float32
bf16 = jnp.bfloat16


def _params(*sem):
    return pltpu.CompilerParams(dimension_semantics=sem, vmem_limit_bytes=V7X_VMEM_LIMIT)


def _sigmoid(x):
    return 1.0 / (1.0 + jnp.exp(-x))


def _ada_kernel(cvt_ref, w_ref, b_ref, o_ref, *, n_rows):
    cvt = cvt_ref[...]
    s = cvt * _sigmoid(cvt)
    w = w_ref[0]
    o_ref[0] = jnp.zeros(o_ref.shape[1:], f32)
    for r in range(n_rows):
        o_ref[0, r:r + 1, :] = jnp.sum(w * s[:, r:r + 1], axis=0, keepdims=True) + b_ref[0]


def _ada(cv, n_rows, w_ada, b_ada):
    depth, d, n6 = w_ada.shape
    tn = min(512, n6)
    return pl.pallas_call(
        functools.partial(_ada_kernel, n_rows=n_rows),
        out_shape=jax.ShapeDtypeStruct((depth, 8, n6), f32),
        grid=(depth, n6 // tn),
        in_specs=[
            pl.BlockSpec((d, 8), lambda l, j: (0, 0)),
            pl.BlockSpec((1, d, tn), lambda l, j: (l, 0, j)),
            pl.BlockSpec((1, 1, tn), lambda l, j: (l, 0, j)),
        ],
        out_specs=pl.BlockSpec((1, 8, tn), lambda l, j: (l, 0, j)),
        compiler_params=_params("arbitrary", "arbitrary"),
        name="ada",
    )(cv.T, w_ada, b_ada.reshape(depth, 1, n6))


def _norm_kernel(*refs, has_resid, has_norm):
    it = iter(refs)
    h_ref = next(it)
    if has_resid:
        f_ref, gate_ref = next(it), next(it)
    if has_norm:
        g_ref, shift_ref, scale_ref = next(it), next(it), next(it)
    h = h_ref[...]
    if has_resid:
        h = h + gate_ref[0] * f_ref[...]
        next(it)[...] = h
    if has_norm:
        ms = jnp.mean(h * h, axis=-1, keepdims=True)
        y = h * lax.rsqrt(ms + EPS) * g_ref[...]
        n_ref = next(it)
        n_ref[...] = (y * (1.0 + scale_ref[0]) + shift_ref[0]).astype(n_ref.dtype)


def _resid_norm(h, rows, seg_of, mods, resid=None, norm=None, n_dtype=bf16):
    t, d = h.shape
    tr = ROW_TILE
    row_spec = pl.BlockSpec((tr, d), lambda i: (i, 0))

    def mod_spec(layer, k):
        return pl.BlockSpec((1, 1, d), lambda i: ((layer * 8 + seg_of(i, tr)) * 6 + k, 0, 0))

    args, in_specs, out_shape, out_specs, aliases = [h], [row_spec], [], [], {}
    if resid is not None:
        f, layer, k_gate = resid
        args += [f, mods]
        in_specs += [row_spec, mod_spec(layer, k_gate)]
        out_shape.append(jax.ShapeDtypeStruct((t if norm is not None else rows, d), f32))
        out_specs.append(row_spec)
        aliases = {0: 0} if norm is not None else {}
    if norm is not None:
        g, layer, k_shift, k_scale = norm
        args += [g.reshape(1, d), mods, mods]
        in_specs += [pl.BlockSpec((1, d), lambda i: (0, 0)), mod_spec(layer, k_shift), mod_spec(layer, k_scale)]
        out_shape.append(jax.ShapeDtypeStruct((t, d), n_dtype))
        out_specs.append(row_spec)
    outs = pl.pallas_call(
        functools.partial(_norm_kernel, has_resid=resid is not None, has_norm=norm is not None),
        out_shape=out_shape,
        grid=(rows // tr,),
        in_specs=in_specs,
        out_specs=out_specs,
        input_output_aliases=aliases,
        compiler_params=_params("arbitrary"),
        name="resid_norm",
    )(*args)
    return outs


def _mm_kernel(a_ref, b_ref, o_ref):
    o_ref[...] = jnp.dot(a_ref[...], b_ref[...], preferred_element_type=f32).astype(o_ref.dtype)


def _matmul(a, b, layer, out_dtype, n=None, tm=512, tn=1024):
    m, k = a.shape
    n = n or b.shape[2]
    tm, tn = min(tm, m), math.gcd(tn, n)
    assert m % tm == 0 and tn % 128 == 0
    return pl.pallas_call(
        _mm_kernel,
        out_shape=jax.ShapeDtypeStruct((m, n), out_dtype),
        grid=(n // tn, m // tm),
        in_specs=[pl.BlockSpec((tm, k), lambda j, i: (i, 0)), pl.BlockSpec((None, k, tn), lambda j, i: (layer, 0, j))],
        out_specs=pl.BlockSpec((tm, tn), lambda j, i: (i, j)),
        compiler_params=_params("arbitrary", "arbitrary"),
        name="in_proj",
    )(a, b)


def _gates_kernel(n_ref, w_ref, b_ref, g_ref, c_ref, *, mh):
    tr = n_ref.shape[0]
    mg = jnp.dot(n_ref[...], w_ref[...], preferred_element_type=f32) + b_ref[...]
    g = GATE_CAP * jnp.tanh(mg * (1.0 / GATE_CAP))
    lane = lax.broadcasted_iota(jnp.int32, g.shape, 1)
    is_forget = (lane // mh) % 2 == 1
    log_sig = jnp.minimum(g, 0.0) - jnp.log(1.0 + jnp.exp(-jnp.abs(g)))
    val = jnp.where(is_forget, log_sig, g)
    g_ref[...] = val
    r = lax.broadcasted_iota(jnp.int32, (M_CHUNK, M_CHUNK), 0)
    c = lax.broadcasted_iota(jnp.int32, (M_CHUNK, M_CHUNK), 1)
    lower = (c <= r).astype(f32)
    upper = (c >= r).astype(f32)
    fwd = lax.broadcasted_iota(jnp.int32, (M_CHUNK, g.shape[1]), 1) < 2 * mh
    for ch in range(tr // M_CHUNK):
        v = val[ch * M_CHUNK:(ch + 1) * M_CHUNK]
        cf = jnp.dot(lower, v, preferred_element_type=f32, precision=lax.Precision.HIGHEST)
        cb = jnp.dot(upper, v, preferred_element_type=f32, precision=lax.Precision.HIGHEST)
        c_ref[ch * M_CHUNK:(ch + 1) * M_CHUNK, :] = jnp.where(fwd, cf, cb)


def _gates(n, w_g, layer, gate_b, mh):
    t, d = n.shape
    ng = 4 * mh
    tr = ROW_TILE
    spec = pl.BlockSpec((tr, ng), lambda i: (i, 0))
    return pl.pallas_call(
        functools.partial(_gates_kernel, mh=mh),
        out_shape=[jax.ShapeDtypeStruct((t, ng), f32)] * 2,
        grid=(t // tr,),
        in_specs=[pl.BlockSpec((tr, d), lambda i: (i, 0)), pl.BlockSpec((None, d, ng), lambda i: (layer, 0, 0)),
                  pl.BlockSpec((1, ng), lambda i: (0, 0))],
        out_specs=[spec, spec],
        compiler_params=_params("arbitrary"),
        name="gates",
    )(n, w_g, gate_b.reshape(1, ng).astype(f32))


def _qk_kernel(x_ref, g_ref, cos_ref, sin_ref, o_ref, *, hd):
    x = x_ref[...].astype(f32)
    cos, sin = cos_ref[...], sin_ref[...]
    r = lax.broadcasted_iota(jnp.int32, (hd, hd), 0)
    c = lax.broadcasted_iota(jnp.int32, (hd, hd), 1)
    swap = jnp.where(r == (c ^ (hd // 4)), 1.0, 0.0).astype(bf16)
    ones = jnp.ones((hd, hd), bf16)
    for s in range(x.shape[1] // hd):
        sl = slice(s * hd, (s + 1) * hd)
        xs = x[:, sl]
        ssq = jnp.dot((xs * xs).astype(bf16), ones, preferred_element_type=f32)
        y = xs * lax.rsqrt(ssq * (1.0 / hd) + EPS) * g_ref[:, sl]
        partner = jnp.dot(y.astype(bf16), swap, preferred_element_type=f32)
        o_ref[:, sl] = (y * cos + partner * sin).astype(o_ref.dtype)


def _qk_prep(parts, gain, cos_t, sin_t, width, n_x_tiles, tiles_per_seq, hd):
    t = parts.shape[0]
    tr = ROW_TILE
    wq = min(1024, width)

    def tbl(i, j):
        return (jnp.where(i < n_x_tiles, i % tiles_per_seq, tiles_per_seq), 0)

    return pl.pallas_call(
        functools.partial(_qk_kernel, hd=hd),
        out_shape=jax.ShapeDtypeStruct((t, width), bf16),
        grid=(t // tr, width // wq),
        in_specs=[pl.BlockSpec((tr, wq), lambda i, j: (i, j)), pl.BlockSpec((1, wq), lambda i, j: (0, j)),
                  pl.BlockSpec((tr, hd), tbl), pl.BlockSpec((tr, hd), tbl)],
        out_specs=pl.BlockSpec((tr, wq), lambda i, j: (i, j)),
        compiler_params=_params("arbitrary", "arbitrary"),
        name="qk_prep",
    )(parts, gain, cos_t, sin_t)


def _attn_kernel(q_ref, kx_ref, kc_ref, vx_ref, vc_ref, lam_ref, bound_ref, g_ref, o_ref, k_ref, vt_ref, *, hd, tkv,
                 tt, nq_x, with_ctx_queries, score_bound):
    i = pl.program_id(2)
    tq = q_ref.shape[0]
    l_ctx, s = kc_ref.shape[0], kx_ref.shape[0]

    @pl.when(i == 0)
    def _():
        k_ref[0:l_ctx] = kc_ref[...]
        vt_ref[:, 0:l_ctx] = vc_ref[...].T
        for c in range(s // tt):
            k_ref[l_ctx + c * tt:l_ctx + (c + 1) * tt] = kx_ref[c * tt:(c + 1) * tt]
            vt_ref[:, l_ctx + c * tt:l_ctx + (c + 1) * tt] = vx_ref[c * tt:(c + 1) * tt].T

    def attend(n_keys, tk):
        qt = q_ref[...].T
        qt0, qt1 = qt[:hd], qt[hd:]

        def scores(c):
            k = k_ref[c * tk:(c + 1) * tk]
            return jnp.concatenate([jnp.dot(k[:, :hd], qt0, preferred_element_type=f32),
                                    jnp.dot(k[:, hd:], qt1, preferred_element_type=f32)], axis=1)

        m = l_sum = acc = None
        st = scores(0)
        for c in range(n_keys // tk):
            st_next = scores(c + 1) if (c + 1) * tk < n_keys else None
            if score_bound:
                m_new = bound_ref[...]
            else:
                m_c = jnp.max(st, axis=0, keepdims=True)
                m_new = m_c if m is None else jnp.maximum(m, m_c)
            p = jnp.exp(st - m_new)
            pv = jnp.dot(vt_ref[:, c * tk:(c + 1) * tk], p.astype(bf16), preferred_element_type=f32)
            if m is None:
                l_sum, acc = jnp.sum(p, axis=0, keepdims=True), pv
            elif score_bound:
                l_sum, acc = l_sum + jnp.sum(p, axis=0, keepdims=True), acc + pv
            else:
                alpha = jnp.exp(m - m_new)
                l_sum, acc = alpha * l_sum + jnp.sum(p, axis=0, keepdims=True), alpha * acc + pv
            m, st = m_new, st_next
        acc = acc / l_sum
        a = (acc[:, :tq] - lam_ref[...] * acc[:, tq:]).T
        ms = jnp.mean(a * a, axis=-1, keepdims=True)
        o_ref[...] = (a * lax.rsqrt(ms + EPS) * g_ref[...]).astype(o_ref.dtype)

    if with_ctx_queries:
        pl.when(i < nq_x)(lambda: attend(l_ctx + s, tkv))
        pl.when(i >= nq_x)(lambda: attend(l_ctx, l_ctx))
    else:
        attend(l_ctx + s, tkv)


def _attention(qk2, parts, lam, bound, out_gain, dims, with_ctx_queries):
    b, s, l, heads, hd = dims
    a_w = heads * 2 * hd
    t = qk2.shape[0]
    tq = ROW_TILE
    n_keys = s + l
    tkv = next(n_keys // n for n in range(1, n_keys) if n_keys % (n * 128) == 0 and n_keys // n <= ATTN_MAX_CHUNK)
    nq_x, nq_c = s // tq, l // tq
    nq = nq_x + (nq_c if with_ctx_queries else 0)
    hw = 2 * hd
    k_col, v_col = a_w // hw, 2 * a_w // hw
    ctx_row = b * s // l

    def q_idx(bi, h, i):
        return (jnp.where(i < nq_x, bi * nq_x + i, b * nq_x + bi * nq_c + (i - nq_x)), h)

    def call(score_bound):
        return pl.pallas_call(
            functools.partial(_attn_kernel, hd=hd, tkv=tkv, tt=min(512, s), nq_x=nq_x,
                              with_ctx_queries=with_ctx_queries, score_bound=score_bound),
            out_shape=jax.ShapeDtypeStruct((t if with_ctx_queries else b * s, a_w), bf16),
            grid=(b, heads, nq),
            in_specs=[
                pl.BlockSpec((tq, hw), q_idx),
                pl.BlockSpec((s, hw), lambda bi, h, i: (bi, k_col + h)),
                pl.BlockSpec((l, hw), lambda bi, h, i: (ctx_row + bi, k_col + h)),
                pl.BlockSpec((s, hw), lambda bi, h, i: (bi, v_col + h)),
                pl.BlockSpec((l, hw), lambda bi, h, i: (ctx_row + bi, v_col + h)),
                pl.BlockSpec((1, 1), lambda bi, h, i: (0, 0)),
                pl.BlockSpec((1, 1), lambda bi, h, i: (0, 0)),
                pl.BlockSpec((1, hw), lambda bi, h, i: (0, 0)),
            ],
            out_specs=pl.BlockSpec((tq, hw), q_idx),
            scratch_shapes=[pltpu.VMEM((n_keys, hw), bf16), pltpu.VMEM((hw, n_keys), bf16)],
            compiler_params=_params("arbitrary", "arbitrary", "arbitrary"),
            name="diff_attn",
        )(qk2, qk2, qk2, parts, parts, lam, bound, out_gain)

    return lax.cond(bound[0, 0] <= ATTN_MAX_BOUND, lambda: call(True), lambda: call(False))


def _mprep_kernel(x_ref, p_ref, n_ref, w_ref, b_ref, sc_ref, o_ref, *, s, l, bs):
    tr = x_ref.shape[0]
    row0 = pl.program_id(0) * tr
    in_x = row0 < bs
    rel = jnp.where(in_x, row0 % s, (row0 - bs) % l)
    seq_len = jnp.where(in_x, s, l)
    keep_prev = jnp.where(rel == 0, 0.0, 1.0)
    keep_next = jnp.where(rel + tr == seq_len, 0.0, 1.0)
    x = x_ref[...].astype(f32)
    prev = p_ref[...].astype(f32)[BF16_ROWS - 1:BF16_ROWS] * keep_prev
    nxt = n_ref[...].astype(f32)[0:1] * keep_next
    rows = lax.broadcasted_iota(jnp.int32, x.shape, 0)
    xm = jnp.where(rows == 0, prev, pltpu.roll(x, 1, 0))
    xp = jnp.where(rows == tr - 1, nxt, pltpu.roll(x, tr - 1, 0))
    w = w_ref[...]
    u = b_ref[...] + xm * w[0:1] + x * w[1:2] + xp * w[2:3]
    o_ref[...] = (u * _sigmoid(u) * sc_ref[...]).astype(o_ref.dtype)


def _mlstm_prep(parts, col0, conv_w, conv_b, post, dims):
    b, s, l = dims
    t = parts.shape[0]
    width = conv_w.shape[1]
    tr = ROW_TILE
    wc = math.gcd(1024, math.gcd(width, col0))
    cb = col0 // wc
    sub = tr // BF16_ROWS
    last = t // BF16_ROWS - 1
    return pl.pallas_call(
        functools.partial(_mprep_kernel, s=s, l=l, bs=b * s),
        out_shape=jax.ShapeDtypeStruct((t, width), bf16),
        grid=(t // tr, width // wc),
        in_specs=[
            pl.BlockSpec((tr, wc), lambda i, j: (i, cb + j)),
            pl.BlockSpec((BF16_ROWS, wc), lambda i, j: (jnp.maximum(i * sub - 1, 0), cb + j)),
            pl.BlockSpec((BF16_ROWS, wc), lambda i, j: (jnp.minimum((i + 1) * sub, last), cb + j)),
            pl.BlockSpec((M_CONV, wc), lambda i, j: (0, j)),
            pl.BlockSpec((1, wc), lambda i, j: (0, j)),
            pl.BlockSpec((1, wc), lambda i, j: (0, j)),
        ],
        out_specs=pl.BlockSpec((tr, wc), lambda i, j: (i, j)),
        compiler_params=_params("arbitrary", "arbitrary"),
        name="mlstm_prep",
    )(parts, parts, parts, conv_w, conv_b, post)


def _scan_chunk(d, q, k, v, li_col, bs_col, li_row, bs_row, c_st, n_st, m_st):
    r = lax.broadcasted_iota(jnp.int32, (M_CHUNK, M_CHUNK), 0)
    c = lax.broadcasted_iota(jnp.int32, (M_CHUNK, M_CHUNK), 1)
    visible = (c <= r) if d == 0 else (c >= r)
    last = M_CHUNK - 1 if d == 0 else 0
    b_end = bs_row[:, last:last + 1]

    d_log = jnp.where(visible, bs_col + (li_row - bs_row), NEG)
    inter = bs_col + m_st
    m_pos = jnp.maximum(jnp.max(d_log, axis=1, keepdims=True), inter)
    w_intra = jnp.exp(d_log - m_pos)
    w_inter = jnp.exp(inter - m_pos)
    sc = lax.dot_general(q, k, (((1,), (1,)), ((), ())), preferred_element_type=f32) * w_intra
    num = jnp.dot(sc.astype(bf16), v, preferred_element_type=f32) + w_inter * jnp.dot(
        q, c_st.astype(bf16), preferred_element_type=f32)
    qn = jnp.sum(q.astype(f32) * n_st, axis=1, keepdims=True)
    den = jnp.sum(sc, axis=1, keepdims=True) + w_inter * qn
    hidden = num / jnp.maximum(jnp.abs(den), jnp.exp(-m_pos))

    w_log = b_end - bs_col + li_col
    m_new = jnp.maximum(b_end + m_st, jnp.max(w_log, axis=0, keepdims=True))
    decay = jnp.exp(b_end + m_st - m_new)
    wk = jnp.exp(w_log - m_new) * k.astype(f32)
    c_new = decay * c_st + lax.dot_general(wk.astype(bf16), v, (((0,), (0,)), ((), ())),
                                           preferred_element_type=f32)
    n_new = decay * n_st + jnp.sum(wk, axis=0, keepdims=True)
    return hidden, c_new, n_new, m_new


def _scan_kernel(*refs, mh, n_sub, n_heads):
    ins, outs, (c_ref, n_ref, m_ref) = refs[:10], refs[10:12], refs[12:]
    ng = 4 * mh
    dk, dv = c_ref.shape[1:]

    @pl.when(pl.program_id(2) == 0)
    def _():
        for ref in (c_ref, n_ref, m_ref):
            ref[...] = jnp.zeros(ref.shape, f32)

    chains = []
    for ci in range(2 * n_heads):
        hh, d = divmod(ci, 2)
        q_ref, k_ref, v_ref, gcol_ref, grow_ref = ins[5 * d:5 * d + 5]
        col_i = d * 2 * mh + pl.program_id(1) * n_heads + hh
        col_f = ng + col_i + mh
        gcol = gcol_ref[...]
        lane = lax.broadcasted_iota(jnp.int32, gcol.shape, 1)
        chains.append(dict(
            d=d, hh=hh, q=q_ref, k=k_ref, v=v_ref,
            li_col=jnp.sum(jnp.where(lane == col_i, gcol, 0.0), axis=1, keepdims=True),
            bs_col=jnp.sum(jnp.where(lane == col_f, gcol, 0.0), axis=1, keepdims=True),
            li_row=grow_ref[pl.ds(col_i, 1), :], bs_row=grow_ref[pl.ds(col_f, 1), :],
            state=(c_ref[ci], n_ref[ci], m_ref[ci]), hidden=[None] * n_sub))
    for j in range(n_sub):
        for ch in chains:
            sub = j if ch["d"] == 0 else n_sub - 1 - j
            rows = slice(sub * M_CHUNK, (sub + 1) * M_CHUNK)
            qk_cols = slice(ch["hh"] * dk, (ch["hh"] + 1) * dk)
            v_cols = slice(ch["hh"] * dv, (ch["hh"] + 1) * dv)
            ch["hidden"][sub], *ch["state"] = _scan_chunk(
                ch["d"], ch["q"][rows, qk_cols], ch["k"][rows, qk_cols], ch["v"][rows, v_cols],
                ch["li_col"][rows], ch["bs_col"][rows], ch["li_row"][:, rows], ch["bs_row"][:, rows], *ch["state"])
    for ch in chains:
        outs[ch["d"]][:, ch["hh"] * dv:(ch["hh"] + 1) * dv] = jnp.concatenate(ch["hidden"], axis=0).astype(bf16)
    for idx, ref in enumerate((c_ref, n_ref, m_ref)):
        ref[...] = jnp.stack([ch["state"][idx] for ch in chains])


def _mlstm_scan(mqk, parts, v_col0, gc_col, gc_row, dims):
    b, s, l, mh, dk, dv = dims
    t = mqk.shape[0]
    rows = SCAN_SUB * M_CHUNK
    assert l % rows == 0 and s % rows == 0
    n_lc, n_sc = l // rows, s // rows
    ctx0 = b * s // rows
    ng2 = gc_col.shape[1]

    def chunk(d, bi, st):
        in_ctx = st < n_lc
        local = jnp.where(in_ctx, st, st - n_lc)
        n_local = jnp.where(in_ctx, n_lc, n_sc)
        pos = local if d == 0 else n_local - 1 - local
        return jnp.where(in_ctx, ctx0 + bi * n_lc, bi * n_sc) + pos

    nh = math.gcd(SCAN_HEADS, mh)
    if v_col0 % (nh * dv):
        nh = 1
    vb = v_col0 // (nh * dv)

    def dir_specs(d):
        return [
            pl.BlockSpec((rows, nh * dk), lambda bi, g, st: (chunk(d, bi, st), g)),
            pl.BlockSpec((rows, nh * dk), lambda bi, g, st: (chunk(d, bi, st), mh // nh + g)),
            pl.BlockSpec((rows, nh * dv), lambda bi, g, st: (chunk(d, bi, st), vb + g)),
            pl.BlockSpec((rows, ng2), lambda bi, g, st: (chunk(d, bi, st), 0)),
            pl.BlockSpec((ng2, rows), lambda bi, g, st: (0, chunk(d, bi, st))),
        ]

    def out_spec(d):
        return pl.BlockSpec((rows, nh * dv), lambda bi, g, st: (chunk(d, bi, st), g))

    state = [pltpu.VMEM((2 * nh, dk, dv), f32), pltpu.VMEM((2 * nh, 1, dk), f32), pltpu.VMEM((2 * nh, 1, 1), f32)]
    args = (mqk, mqk, parts, gc_col, gc_row)
    return pl.pallas_call(
        functools.partial(_scan_kernel, mh=mh, n_sub=SCAN_SUB, n_heads=nh),
        out_shape=[jax.ShapeDtypeStruct((t, mh * dv), bf16)] * 2,
        grid=(b, mh // nh, n_lc + n_sc),
        in_specs=dir_specs(0) + dir_specs(1),
        out_specs=[out_spec(0), out_spec(1)],
        scratch_shapes=state,
        compiler_params=_params("arbitrary", "arbitrary", "arbitrary"),
        name="mlstm_scan",
    )(*args, *args)


def _mout_kernel(hf_ref, hb_ref, o_ref, g_ref, out_ref, *, dv):
    for hh in range(out_ref.shape[1] // dv):
        cols = slice(hh * dv, (hh + 1) * dv)
        hs = hf_ref[:, cols].astype(f32) + hb_ref[:, cols].astype(f32)
        ms = jnp.mean(hs * hs, axis=-1, keepdims=True)
        y = hs * lax.rsqrt(ms + EPS) * g_ref[:, cols]
        out_ref[:, cols] = (y * _sigmoid(o_ref[:, cols].astype(f32))).astype(out_ref.dtype)


def _mlstm_out(h_fwd, h_bwd, parts, o_col0, head_g, rows):
    m_w = h_fwd.shape[1]
    mh, dv = head_g.shape
    tr = ROW_TILE
    wb = math.gcd(MOUT_HEADS * dv, math.gcd(m_w, o_col0))
    ob = o_col0 // wb
    head_spec = pl.BlockSpec((tr, wb), lambda i, g: (i, g))
    return pl.pallas_call(
        functools.partial(_mout_kernel, dv=dv),
        out_shape=jax.ShapeDtypeStruct((rows, m_w), bf16),
        grid=(rows // tr, m_w // wb),
        in_specs=[head_spec, head_spec, pl.BlockSpec((tr, wb), lambda i, g: (i, ob + g)),
                  pl.BlockSpec((1, wb), lambda i, g: (0, g))],
        out_specs=head_spec,
        compiler_params=_params("arbitrary", "arbitrary"),
        name="mlstm_out",
    )(h_fwd, h_bwd, parts, head_g.reshape(1, m_w).astype(f32))


def _merge_kernel(a_ref, wa_ref, m_ref, wm_ref, ga_ref, gb_ref, o_ref):
    ya = jnp.dot(a_ref[...], wa_ref[...], preferred_element_type=f32)
    ym = jnp.dot(m_ref[...], wm_ref[...], preferred_element_type=f32)
    y = _sigmoid(ga_ref[...].astype(f32)) * ya + _sigmoid(gb_ref[...].astype(f32)) * ym
    o_ref[...] = y.astype(o_ref.dtype)


def _merge(a_flat, m_flat, w_a, w_m, layer, parts, ga_col0, rows):
    d = w_a.shape[2]
    tm, tn = 512, min(512, d)
    ga, gb = ga_col0 // tn, (ga_col0 + d) // tn
    return pl.pallas_call(
        _merge_kernel,
        out_shape=jax.ShapeDtypeStruct((rows, d), bf16),
        grid=(d // tn, rows // tm),
        in_specs=[
            pl.BlockSpec((tm, a_flat.shape[1]), lambda j, i: (i, 0)),
            pl.BlockSpec((None, w_a.shape[1], tn), lambda j, i: (layer, 0, j)),
            pl.BlockSpec((tm, m_flat.shape[1]), lambda j, i: (i, 0)),
            pl.BlockSpec((None, w_m.shape[1], tn), lambda j, i: (layer, 0, j)),
            pl.BlockSpec((tm, tn), lambda j, i: (i, ga + j)),
            pl.BlockSpec((tm, tn), lambda j, i: (i, gb + j)),
        ],
        out_specs=pl.BlockSpec((tm, tn), lambda j, i: (i, j)),
        compiler_params=_params("arbitrary", "arbitrary"),
        name="branch_merge",
    )(a_flat, w_a, m_flat, w_m, parts, parts)


def _outproj_kernel(y_ref, w_ref, h_ref, gate_ref, o_ref):
    o_ref[...] = h_ref[...] + gate_ref[0] * jnp.dot(y_ref[...], w_ref[...], preferred_element_type=f32)


def _out_proj(y, w_out, h, mods, layer, seg_of, rows):
    t, d = h.shape
    tm, tn = 512, min(512, d)
    return pl.pallas_call(
        _outproj_kernel,
        out_shape=jax.ShapeDtypeStruct((t, d), f32),
        grid=(d // tn, rows // tm),
        in_specs=[
            pl.BlockSpec((tm, d), lambda j, i: (i, 0)),
            pl.BlockSpec((None, d, tn), lambda j, i: (layer, 0, j)),
            pl.BlockSpec((tm, tn), lambda j, i: (i, j)),
            pl.BlockSpec((1, 1, tn), lambda j, i: ((layer * 8 + seg_of(i, tm)) * 6 + 2, 0, j)),
        ],
        out_specs=pl.BlockSpec((tm, tn), lambda j, i: (i, j)),
        input_output_aliases={2: 0},
        compiler_params=_params("arbitrary", "arbitrary"),
        name="out_proj",
    )(y, w_out, h, mods)


def _route_kernel(h_ref, g_ref, shift_ref, scale_ref, wr_ref, bias_ref, n_ref, r_ref):
    h = h_ref[...]
    ms = jnp.mean(h * h, axis=-1, keepdims=True)
    n = h * lax.rsqrt(ms + EPS) * g_ref[...] * (1.0 + scale_ref[0]) + shift_ref[0]
    n_ref[...] = n
    logits = lax.dot_general(wr_ref[...], n, (((1,), (1,)), ((), ())), preferred_element_type=f32,
                             precision=lax.Precision.HIGHEST)
    score = _sigmoid(logits)
    sel = score + bias_ref[...]
    sel_e = [sel[e:e + 1] for e in range(N_GROUPS * GROUP_SIZE)]
    score_e = [score[e:e + 1] for e in range(N_GROUPS * GROUP_SIZE)]

    def top2_sum(v):
        best = v[0] + v[1]
        for a in range(GROUP_SIZE):
            for b in range(a + 1, GROUP_SIZE):
                if (a, b) != (0, 1):
                    best = jnp.maximum(best, v[a] + v[b])
        return best

    g_best = top2_sum(sel_e[:GROUP_SIZE])
    grp = jnp.zeros(g_best.shape, jnp.int32)
    for g in range(1, N_GROUPS):
        cand = top2_sum(sel_e[g * GROUP_SIZE:(g + 1) * GROUP_SIZE])
        better = cand > g_best
        grp = jnp.where(better, g, grp)
        g_best = jnp.where(better, cand, g_best)

    def in_group(rows):
        out = []
        for j in range(GROUP_SIZE):
            v = rows[j]
            for g in range(1, N_GROUPS):
                v = jnp.where(grp == g, rows[g * GROUP_SIZE + j], v)
            out.append(v)
        return out

    v, sc = in_group(sel_e), in_group(score_e)
    j1, b1, c1 = jnp.zeros(grp.shape, jnp.int32), v[0], sc[0]
    for j in range(1, GROUP_SIZE):
        better = v[j] > b1
        j1, b1, c1 = jnp.where(better, j, j1), jnp.where(better, v[j], b1), jnp.where(better, sc[j], c1)
    j2 = jnp.full(grp.shape, -1, jnp.int32)
    b2, c2 = jnp.full(b1.shape, -jnp.inf, f32), jnp.zeros(b1.shape, f32)
    for j in range(GROUP_SIZE):
        better = (j1 != j) & ((v[j] > b2) | (j2 < 0))
        j2, b2, c2 = jnp.where(better, j, j2), jnp.where(better, v[j], b2), jnp.where(better, sc[j], c2)
    tot = c1 + c2
    w1, w2 = c1 / tot, c2 / tot
    first_low = j1 < j2
    lo, hi = jnp.minimum(j1, j2), jnp.maximum(j1, j2)
    pair_in_group = jnp.zeros(lo.shape, jnp.int32)
    for idx, (p_lo, p_hi) in enumerate(PAIR_ORDER):
        pair_in_group = jnp.where((lo == p_lo) & (hi == p_hi), idx, pair_in_group)
    r_ref[...] = jnp.zeros(r_ref.shape, f32)
    r_ref[0:1, :] = (grp * 6 + pair_in_group).astype(f32)
    r_ref[1:2, :] = jnp.where(first_low, w1, w2)
    r_ref[2:3, :] = jnp.where(first_low, w2, w1)


def _norm_route(h, g, mods, layer, seg_of, w_router_t, bias, rows):
    t, d = h.shape
    e = w_router_t.shape[0]
    tr = ROW_TILE

    def mod_spec(k):
        return pl.BlockSpec((1, 1, d), lambda i: ((layer * 8 + seg_of(i, tr)) * 6 + k, 0, 0))

    return pl.pallas_call(
        _route_kernel,
        out_shape=[jax.ShapeDtypeStruct((rows, d), f32), jax.ShapeDtypeStruct((8, rows), f32)],
        grid=(rows // tr,),
        in_specs=[pl.BlockSpec((tr, d), lambda i: (i, 0)), pl.BlockSpec((1, d), lambda i: (0, 0)),
                  mod_spec(3), mod_spec(4), pl.BlockSpec((e, d), lambda i: (0, 0)),
                  pl.BlockSpec((e, 1), lambda i: (0, 0))],
        out_specs=[pl.BlockSpec((tr, d), lambda i: (i, 0)), pl.BlockSpec((8, tr), lambda i: (0, i))],
        compiler_params=_params("arbitrary"),
        name="norm_route",
    )(h, g.reshape(1, d), mods, mods, w_router_t, bias.reshape(e, 1).astype(f32))


def _moe_kernel(lo_ref, hi_ref, nv_ref, tok_ref, x_hbm, w_ref, wg_lo, wu_lo, wd_lo, wg_hi, wu_hi, wd_hi,
                out_hbm, xbuf, ybuf, sem_in, sem_out):
    blk = pl.program_id(0)
    slot = blk % 2

    def row_in(b, r):
        return pltpu.make_async_copy(x_hbm.at[pl.ds(tok_ref[b * MOE_BLOCK + r], 1)],
                                     xbuf.at[b % 2, pl.ds(r, 1)], sem_in.at[b % 2])

    def row_out(b, r):
        return pltpu.make_async_copy(ybuf.at[b % 2, pl.ds(r, 1)],
                                     out_hbm.at[pl.ds(tok_ref[b * MOE_BLOCK + r], 1)], sem_out.at[b % 2])

    def for_rows(fn):
        def body(r, c):
            fn(r)
            return c

        lax.fori_loop(0, MOE_BLOCK, body, 0, unroll=8)

    def start_out(b):
        for_rows(lambda r: pl.when(r < nv_ref[b])(lambda: row_out(b, r).start()))

    def wait_in(b):
        pltpu.make_async_copy(x_hbm.at[pl.ds(0, MOE_BLOCK)], xbuf.at[b % 2], sem_in.at[b % 2]).wait()

    def wait_out(b):
        nv = nv_ref[b]
        p = 1
        while p <= MOE_BLOCK:
            rows = pl.ds(0, p)
            pl.when((nv & p) != 0)(
                pltpu.make_async_copy(ybuf.at[b % 2, rows], out_hbm.at[rows], sem_out.at[b % 2]).wait)
            p *= 2

    def has_rows(b):
        n_blk = pl.num_programs(0)
        return jnp.logical_and(b < n_blk, nv_ref[jnp.minimum(b, n_blk - 1)] > 0)

    @pl.when(nv_ref[blk] > 0)
    def _():
        @pl.when(blk == 0)
        def _():
            for_rows(lambda r: row_in(blk, r).start())

        wait_in(blk)

        @pl.when(has_rows(blk + 1))
        def _():
            for_rows(lambda r: row_in(blk + 1, r).start())

        x = xbuf[slot].astype(bf16)
        eye = (lax.broadcasted_iota(jnp.int32, (MOE_BLOCK, MOE_BLOCK), 0)
               == lax.broadcasted_iota(jnp.int32, (MOE_BLOCK, MOE_BLOCK), 1))
        w = w_ref[0]

        def expert(wg, wu, wd, w_row):
            a = jnp.dot(x, wg[0], preferred_element_type=f32)
            u = jnp.dot(x, wu[0], preferred_element_type=f32)
            y = jnp.dot((a * _sigmoid(a) * u).astype(bf16), wd[0], preferred_element_type=f32)
            return y * jnp.sum(jnp.where(eye, w_row, 0.0), axis=1, keepdims=True)

        ybuf[slot] = expert(wg_lo, wu_lo, wd_lo, w[0:1]) + expert(wg_hi, wu_hi, wd_hi, w[1:2])

        @pl.when(blk > 0)
        def _():
            wait_out(blk - 1)

        start_out(blk)

        @pl.when(jnp.logical_not(has_rows(blk + 1)))
        def _():
            wait_out(blk)


def _moe(n2, route, w_gate, w_up, w_down, layer, n_tok):
    d = n2.shape[1]
    dff = w_gate.shape[-1]
    blk = MOE_BLOCK
    n_blk = -(-n_tok // blk) + N_PAIRS
    n_pad = n_blk * blk
    pair = route[0, :n_tok].astype(jnp.int32)
    onehot = (pair[:, None] == jnp.arange(N_PAIRS, dtype=jnp.int32)[None, :]).astype(jnp.int32)
    csum = jnp.cumsum(onehot, axis=0)
    rank = jnp.sum(csum * onehot, axis=1) - 1
    counts = csum[-1]
    padded = (counts + blk - 1) // blk * blk
    pad_end = jnp.cumsum(padded)
    pad_start = pad_end - padded
    dest = pad_start[pair] + rank
    buf_tok = jnp.zeros((n_pad,), jnp.int32).at[dest].set(jnp.arange(n_tok, dtype=jnp.int32))
    wts = jnp.zeros((2, n_pad), f32).at[:, dest].set(route[1:3, :n_tok])
    wts = wts.reshape(2, n_blk, blk).transpose(1, 0, 2)
    blk_start = jnp.arange(n_blk, dtype=jnp.int32) * blk
    seg = jnp.minimum(jnp.searchsorted(pad_end, blk_start, side="right"), N_PAIRS - 1).astype(jnp.int32)
    nv = jnp.clip(counts[seg] - (blk_start - pad_start[seg]), 0, blk).astype(jnp.int32)
    pair_tbl = jnp.asarray(PAIR_ORDER, jnp.int32)
    e_lo = (seg // len(PAIR_ORDER)) * GROUP_SIZE + pair_tbl[seg % len(PAIR_ORDER), 0]
    e_hi = (seg // len(PAIR_ORDER)) * GROUP_SIZE + pair_tbl[seg % len(PAIR_ORDER), 1]

    def w_in(sel):
        return pl.BlockSpec((None, 1, d, dff), lambda i, lo, hi, nvr, tok: (layer, (lo, hi)[sel][i], 0, 0),
                            pipeline_mode=pl.Buffered(1))

    def w_out(sel):
        return pl.BlockSpec((None, 1, dff, d), lambda i, lo, hi, nvr, tok: (layer, (lo, hi)[sel][i], 0, 0),
                            pipeline_mode=pl.Buffered(1))

    return pl.pallas_call(
        _moe_kernel,
        out_shape=jax.ShapeDtypeStruct((n_tok, d), f32),
        grid_spec=pltpu.PrefetchScalarGridSpec(
            num_scalar_prefetch=4,
            grid=(n_blk,),
            in_specs=[pl.BlockSpec(memory_space=pl.ANY),
                      pl.BlockSpec((1, 2, blk), lambda i, lo, hi, nvr, tok: (i, 0, 0)),
                      w_in(0), w_in(0), w_out(0), w_in(1), w_in(1), w_out(1)],
            out_specs=pl.BlockSpec(memory_space=pl.ANY),
            scratch_shapes=[pltpu.VMEM((2, blk, d), f32), pltpu.VMEM((2, blk, d), f32),
                            pltpu.SemaphoreType.DMA((2,)), pltpu.SemaphoreType.DMA((2,))],
        ),
        compiler_params=_params("arbitrary"),
        name="moe",
    )(e_lo, e_hi, nv, buf_tok, n2, wts, w_gate, w_up, w_down, w_gate, w_up, w_down)


def _rope_tables(n_rows, hd, pad_rows):
    n_freq = hd // 4
    inv_freq = ROPE_THETA ** (-jnp.arange(n_freq, dtype=f32) / n_freq)
    pos = jnp.arange(n_rows * GRID_W)
    ang_row = (pos // GRID_W).astype(f32)[:, None] * inv_freq
    ang_col = (pos % GRID_W).astype(f32)[:, None] * inv_freq
    ang = jnp.concatenate([ang_row, ang_row, ang_col, ang_col], axis=1)
    sign = jnp.tile(jnp.concatenate([-jnp.ones((n_freq,), f32), jnp.ones((n_freq,), f32)]), 2)
    cos_t = jnp.concatenate([jnp.cos(ang), jnp.ones((pad_rows, hd), f32)], axis=0)
    sin_t = jnp.concatenate([jnp.sin(ang) * sign, jnp.zeros((pad_rows, hd), f32)], axis=0)
    return cos_t, sin_t


def kernel(x, c, ctx, c_ctx, w_ada, b_ada, g_norm1, g_norm2, w_in, q_norm_g, k_norm_g, diff_lambda, attn_head_g,
           conv_w, conv_b, mlstm_gate_b, mlstm_head_g, w_br_attn, w_br_mlstm, w_out, w_router, router_bias,
           w_gate, w_up, w_down):
    b, s, d = x.shape
    l = ctx.shape[1]
    depth = w_ada.shape[0]
    hd = q_norm_g.shape[-1]
    a_w = w_br_attn.shape[1]
    heads = a_w // (2 * hd)
    mh, dv = mlstm_head_g.shape[-2:]
    qk_w = conv_w.shape[-1] // 2
    dk = qk_w // mh
    m_w = mh * dv
    ng = 4 * mh
    assert w_router.shape[1] == N_GROUPS * GROUP_SIZE
    assert s % ROW_TILE == 0 and l % ROW_TILE == 0 and s % GRID_W == 0 and b + 1 <= 8
    t_x, t_all = b * s, b * s + b * l
    gate_col = 3 * a_w + 2 * qk_w + 2 * m_w
    mq_col, mv_col, mo_col = 3 * a_w, 3 * a_w + 2 * qk_w, 3 * a_w + 2 * qk_w + m_w

    def seg_of(i, tile):
        return jnp.where(i < t_x // tile, i // (s // tile), b)

    w_in_bf = w_in.astype(bf16)
    w_g = w_in_bf[:, :, gate_col:gate_col + ng]
    w_bg = w_in_bf[:, :, gate_col + ng:]
    w_a, w_m, w_o = w_br_attn.astype(bf16), w_br_mlstm.astype(bf16), w_out.astype(bf16)
    w_eg, w_eu, w_ed = w_gate.astype(bf16), w_up.astype(bf16), w_down.astype(bf16)
    cv = jnp.zeros((8, d), f32).at[:b].set(c).at[b].set(c_ctx)
    mods = _ada(cv, b + 1, w_ada, b_ada).reshape(depth * 8 * 6, 1, d)
    cos_t, sin_t = _rope_tables(s // GRID_W, hd, ROW_TILE)
    h = jnp.concatenate([x.reshape(t_x, d), ctx.reshape(b * l, d)], axis=0)
    f = None
    for layer in range(depth):
        need_ctx = layer < depth - 1
        rows = t_all if need_ctx else t_x
        lam_init = 0.8 - 0.6 * math.exp(-0.3 * layer)
        lv = diff_lambda[layer].astype(f32)
        lam = (jnp.exp(jnp.sum(lv[0] * lv[1])) - jnp.exp(jnp.sum(lv[2] * lv[3])) + lam_init).reshape(1, 1)
        if f is None:
            (n1,) = _resid_norm(h, t_all, seg_of, mods, norm=(g_norm1[layer], layer, 0, 1))
        else:
            h, n1 = _resid_norm(h, t_all, seg_of, mods, resid=(f, layer - 1, 5), norm=(g_norm1[layer], layer, 0, 1))
        parts = _matmul(n1, w_in_bf, layer, bf16, n=gate_col)
        branch_gates = _matmul(n1, w_bg, layer, bf16)
        g_col, c_col = _gates(n1, w_g, layer, mlstm_gate_b[layer], mh)

        qk_gain = jnp.concatenate([jnp.tile(q_norm_g[layer].astype(f32), 2 * heads) * hd ** -0.5,
                                   jnp.tile(k_norm_g[layer].astype(f32), 2 * heads)]).reshape(1, 2 * a_w)
        qk2 = _qk_prep(parts, qk_gain, cos_t, sin_t, 2 * a_w, t_x // ROW_TILE, s // ROW_TILE, hd)
        out_gain = (attn_head_g[layer].astype(f32) * (1.0 - lam_init)).reshape(1, 2 * hd)
        bound = (1.01 * hd ** 0.5 * jnp.max(jnp.abs(q_norm_g[layer])) * jnp.max(jnp.abs(k_norm_g[layer])))
        a_flat = _attention(qk2, parts, lam, bound.astype(f32).reshape(1, 1), out_gain, (b, s, l, heads, hd), need_ctx)

        post = jnp.concatenate([jnp.ones((qk_w,), f32), jnp.full((qk_w,), dk ** -0.5, f32)]).reshape(1, 2 * qk_w)
        mqk = _mlstm_prep(parts, mq_col, conv_w[layer].astype(f32), conv_b[layer].reshape(1, -1).astype(f32), post,
                          (b, s, l))
        gc_col = jnp.concatenate([g_col, c_col], axis=1)
        h_fwd, h_bwd = _mlstm_scan(mqk, parts, mv_col, gc_col, gc_col.T, (b, s, l, mh, dk, dv))
        m_flat = _mlstm_out(h_fwd, h_bwd, parts, mo_col, mlstm_head_g[layer], rows)

        y = _merge(a_flat, m_flat, w_a, w_m, layer, branch_gates, 0, rows)
        h = _out_proj(y, w_o, h, mods, layer, seg_of, rows)
        n2, route = _norm_route(h, g_norm2[layer], mods, layer, seg_of, w_router.T.astype(f32), router_bias, rows)
        f = _moe(n2, route, w_eg, w_eu, w_ed, layer, rows)
    (h,) = _resid_norm(h, t_x, seg_of, mods, resid=(f, depth - 1, 5))
    return h.reshape(b, s, d)
```

```python
import functools
import math

import jax
import jax.numpy as jnp
from jax import lax
from jax.experimental import pallas as pl
from jax.experimental.pallas import tpu as pltpu

GRID_W = 64
EPS = 1e-6
ROPE_THETA = 10000.0
M_CHUNK = 128
M_CONV = 3
GATE_CAP = 15.0
N_GROUPS = 4
GROUP_SIZE = 4
PAIR_ORDER = ((0, 1), (0, 2), (1, 2), (1, 3), (0, 3), (2, 3))
N_PAIRS = N_GROUPS * len(PAIR_ORDER)
NEG = -1e30

V7X_VMEM_LIMIT = 56 * 1024 * 1024
ROW_TILE = 256
MOE_BLOCK = 128
BF16_ROWS = 16
SCAN_SUB = 2
SCAN_HEADS = 2
MOUT_HEADS = 2
ATTN_MAX_CHUNK = 3072
ATTN_MAX_BOUND = 40.0

f32 = jnp.float32
bf16 = jnp.bfloat16


def _params(*sem):
    return pltpu.CompilerParams(dimension_semantics=sem, vmem_limit_bytes=V7X_VMEM_LIMIT)


def _sigmoid(x):
    return 1.0 / (1.0 + jnp.exp(-x))


def _ada_kernel(cvt_ref, w_ref, b_ref, o_ref, *, n_rows):
    cvt = cvt_ref[...]
    s = cvt * _sigmoid(cvt)
    w = w_ref[0]
    o_ref[0] = jnp.zeros(o_ref.shape[1:], f32)
    for r in range(n_rows):
        o_ref[0, r:r + 1, :] = jnp.sum(w * s[:, r:r + 1], axis=0, keepdims=True) + b_ref[0]


def _ada(cv, n_rows, w_ada, b_ada):
    depth, d, n6 = w_ada.shape
    tn = min(512, n6)
    return pl.pallas_call(
        functools.partial(_ada_kernel, n_rows=n_rows),
        out_shape=jax.ShapeDtypeStruct((depth, 8, n6), f32),
        grid=(depth, n6 // tn),
        in_specs=[
            pl.BlockSpec((d, 8), lambda l, j: (0, 0)),
            pl.BlockSpec((1, d, tn), lambda l, j: (l, 0, j)),
            pl.BlockSpec((1, 1, tn), lambda l, j: (l, 0, j)),
        ],
        out_specs=pl.BlockSpec((1, 8, tn), lambda l, j: (l, 0, j)),
        compiler_params=_params("arbitrary", "arbitrary"),
        name="ada",
    )(cv.T, w_ada, b_ada.reshape(depth, 1, n6))


def _norm_kernel(*refs, has_resid, has_norm):
    it = iter(refs)
    h_ref = next(it)
    if has_resid:
        f_ref, gate_ref = next(it), next(it)
    if has_norm:
        g_ref, shift_ref, scale_ref = next(it), next(it), next(it)
    h = h_ref[...]
    if has_resid:
        h = h + gate_ref[0] * f_ref[...]
        next(it)[...] = h
    if has_norm:
        ms = jnp.mean(h * h, axis=-1, keepdims=True)
        y = h * lax.rsqrt(ms + EPS) * g_ref[...]
        n_ref = next(it)
        n_ref[...] = (y * (1.0 + scale_ref[0]) + shift_ref[0]).astype(n_ref.dtype)


def _resid_norm(h, rows, seg_of, mods, resid=None, norm=None, n_dtype=bf16):
    t, d = h.shape
    tr = ROW_TILE
    row_spec = pl.BlockSpec((tr, d), lambda i: (i, 0))

    def mod_spec(layer, k):
        return pl.BlockSpec((1, 1, d), lambda i: ((layer * 8 + seg_of(i, tr)) * 6 + k, 0, 0))

    args, in_specs, out_shape, out_specs, aliases = [h], [row_spec], [], [], {}
    if resid is not None:
        f, layer, k_gate = resid
        args += [f, mods]
        in_specs += [row_spec, mod_spec(layer, k_gate)]
        out_shape.append(jax.ShapeDtypeStruct((t if norm is not None else rows, d), f32))
        out_specs.append(row_spec)
        aliases = {0: 0} if norm is not None else {}
    if norm is not None:
        g, layer, k_shift, k_scale = norm
        args += [g.reshape(1, d), mods, mods]
        in_specs += [pl.BlockSpec((1, d), lambda i: (0, 0)), mod_spec(layer, k_shift), mod_spec(layer, k_scale)]
        out_shape.append(jax.ShapeDtypeStruct((t, d), n_dtype))
        out_specs.append(row_spec)
    outs = pl.pallas_call(
        functools.partial(_norm_kernel, has_resid=resid is not None, has_norm=norm is not None),
        out_shape=out_shape,
        grid=(rows // tr,),
        in_specs=in_specs,
        out_specs=out_specs,
        input_output_aliases=aliases,
        compiler_params=_params("arbitrary"),
        name="resid_norm",
    )(*args)
    return outs


def _mm_kernel(a_ref, b_ref, o_ref):
    o_ref[...] = jnp.dot(a_ref[...], b_ref[...], preferred_element_type=f32).astype(o_ref.dtype)


def _matmul(a, b, layer, out_dtype, n=None, tm=512, tn=1024):
    m, k = a.shape
    n = n or b.shape[2]
    tm, tn = min(tm, m), math.gcd(tn, n)
    assert m % tm == 0 and tn % 128 == 0
    return pl.pallas_call(
        _mm_kernel,
        out_shape=jax.ShapeDtypeStruct((m, n), out_dtype),
        grid=(n // tn, m // tm),
        in_specs=[pl.BlockSpec((tm, k), lambda j, i: (i, 0)), pl.BlockSpec((None, k, tn), lambda j, i: (layer, 0, j))],
        out_specs=pl.BlockSpec((tm, tn), lambda j, i: (i, j)),
        compiler_params=_params("arbitrary", "arbitrary"),
        name="in_proj",
    )(a, b)


def _gates_kernel(n_ref, w_ref, b_ref, g_ref, c_ref, *, mh):
    tr = n_ref.shape[0]
    mg = jnp.dot(n_ref[...], w_ref[...], preferred_element_type=f32) + b_ref[...]
    g = GATE_CAP * jnp.tanh(mg * (1.0 / GATE_CAP))
    lane = lax.broadcasted_iota(jnp.int32, g.shape, 1)
    is_forget = (lane // mh) % 2 == 1
    log_sig = jnp.minimum(g, 0.0) - jnp.log(1.0 + jnp.exp(-jnp.abs(g)))
    val = jnp.where(is_forget, log_sig, g)
    g_ref[...] = val
    r = lax.broadcasted_iota(jnp.int32, (M_CHUNK, M_CHUNK), 0)
    c = lax.broadcasted_iota(jnp.int32, (M_CHUNK, M_CHUNK), 1)
    lower = (c <= r).astype(f32)
    upper = (c >= r).astype(f32)
    fwd = lax.broadcasted_iota(jnp.int32, (M_CHUNK, g.shape[1]), 1) < 2 * mh
    for ch in range(tr // M_CHUNK):
        v = val[ch * M_CHUNK:(ch + 1) * M_CHUNK]
        cf = jnp.dot(lower, v, preferred_element_type=f32, precision=lax.Precision.HIGHEST)
        cb = jnp.dot(upper, v, preferred_element_type=f32, precision=lax.Precision.HIGHEST)
        c_ref[ch * M_CHUNK:(ch + 1) * M_CHUNK, :] = jnp.where(fwd, cf, cb)


def _gates(n, w_g, layer, gate_b, mh):
    t, d = n.shape
    ng = 4 * mh
    tr = ROW_TILE
    spec = pl.BlockSpec((tr, ng), lambda i: (i, 0))
    return pl.pallas_call(
        functools.partial(_gates_kernel, mh=mh),
        out_shape=[jax.ShapeDtypeStruct((t, ng), f32)] * 2,
        grid=(t // tr,),
        in_specs=[pl.BlockSpec((tr, d), lambda i: (i, 0)), pl.BlockSpec((None, d, ng), lambda i: (layer, 0, 0)),
                  pl.BlockSpec((1, ng), lambda i: (0, 0))],
        out_specs=[spec, spec],
        compiler_params=_params("arbitrary"),
        name="gates",
    )(n, w_g, gate_b.reshape(1, ng).astype(f32))


def _qk_kernel(x_ref, g_ref, cos_ref, sin_ref, o_ref, *, hd):
    x = x_ref[...].astype(f32)
    cos, sin = cos_ref[...], sin_ref[...]
    r = lax.broadcasted_iota(jnp.int32, (hd, hd), 0)
    c = lax.broadcasted_iota(jnp.int32, (hd, hd), 1)
    swap = jnp.where(r == (c ^ (hd // 4)), 1.0, 0.0).astype(bf16)
    ones = jnp.ones((hd, hd), bf16)
    for s in range(x.shape[1] // hd):
        sl = slice(s * hd, (s + 1) * hd)
        xs = x[:, sl]
        ssq = jnp.dot((xs * xs).astype(bf16), ones, preferred_element_type=f32)
        y = xs * lax.rsqrt(ssq * (1.0 / hd) + EPS) * g_ref[:, sl]
        partner = jnp.dot(y.astype(bf16), swap, preferred_element_type=f32)
        o_ref[:, sl] = (y * cos + partner * sin).astype(o_ref.dtype)


def _qk_prep(parts, gain, cos_t, sin_t, width, n_x_tiles, tiles_per_seq, hd):
    t = parts.shape[0]
    tr = ROW_TILE
    wq = min(1024, width)

    def tbl(i, j):
        return (jnp.where(i < n_x_tiles, i % tiles_per_seq, tiles_per_seq), 0)

    return pl.pallas_call(
        functools.partial(_qk_kernel, hd=hd),
        out_shape=jax.ShapeDtypeStruct((t, width), bf16),
        grid=(t // tr, width // wq),
        in_specs=[pl.BlockSpec((tr, wq), lambda i, j: (i, j)), pl.BlockSpec((1, wq), lambda i, j: (0, j)),
                  pl.BlockSpec((tr, hd), tbl), pl.BlockSpec((tr, hd), tbl)],
        out_specs=pl.BlockSpec((tr, wq), lambda i, j: (i, j)),
        compiler_params=_params("arbitrary", "arbitrary"),
        name="qk_prep",
    )(parts, gain, cos_t, sin_t)


def _attn_kernel(q_ref, kx_ref, kc_ref, vx_ref, vc_ref, lam_ref, bound_ref, g_ref, o_ref, k_ref, vt_ref, *, hd, tkv,
                 tt, nq_x, with_ctx_queries, score_bound):
    i = pl.program_id(2)
    tq = q_ref.shape[0]
    l_ctx, s = kc_ref.shape[0], kx_ref.shape[0]

    @pl.when(i == 0)
    def _():
        k_ref[0:l_ctx] = kc_ref[...]
        vt_ref[:, 0:l_ctx] = vc_ref[...].T
        for c in range(s // tt):
            k_ref[l_ctx + c * tt:l_ctx + (c + 1) * tt] = kx_ref[c * tt:(c + 1) * tt]
            vt_ref[:, l_ctx + c * tt:l_ctx + (c + 1) * tt] = vx_ref[c * tt:(c + 1) * tt].T

    def attend(n_keys, tk):
        qt = q_ref[...].T
        qt0, qt1 = qt[:hd], qt[hd:]

        def scores(c):
            k = k_ref[c * tk:(c + 1) * tk]
            return jnp.concatenate([jnp.dot(k[:, :hd], qt0, preferred_element_type=f32),
                                    jnp.dot(k[:, hd:], qt1, preferred_element_type=f32)], axis=1)

        m = l_sum = acc = None
        st = scores(0)
        for c in range(n_keys // tk):
            st_next = scores(c + 1) if (c + 1) * tk < n_keys else None
            if score_bound:
                m_new = bound_ref[...]
            else:
                m_c = jnp.max(st, axis=0, keepdims=True)
                m_new = m_c if m is None else jnp.maximum(m, m_c)
            p = jnp.exp(st - m_new)
            pv = jnp.dot(vt_ref[:, c * tk:(c + 1) * tk], p.astype(bf16), preferred_element_type=f32)
            if m is None:
                l_sum, acc = jnp.sum(p, axis=0, keepdims=True), pv
            elif score_bound:
                l_sum, acc = l_sum + jnp.sum(p, axis=0, keepdims=True), acc + pv
            else:
                alpha = jnp.exp(m - m_new)
                l_sum, acc = alpha * l_sum + jnp.sum(p, axis=0, keepdims=True), alpha * acc + pv
            m, st = m_new, st_next
        acc = acc / l_sum
        a = (acc[:, :tq] - lam_ref[...] * acc[:, tq:]).T
        ms = jnp.mean(a * a, axis=-1, keepdims=True)
        o_ref[...] = (a * lax.rsqrt(ms + EPS) * g_ref[...]).astype(o_ref.dtype)

    if with_ctx_queries:
        pl.when(i < nq_x)(lambda: attend(l_ctx + s, tkv))
        pl.when(i >= nq_x)(lambda: attend(l_ctx, l_ctx))
    else:
        attend(l_ctx + s, tkv)


def _attention(qk2, parts, lam, bound, out_gain, dims, with_ctx_queries):
    b, s, l, heads, hd = dims
    a_w = heads * 2 * hd
    t = qk2.shape[0]
    tq = ROW_TILE
    n_keys = s + l
    tkv = next(n_keys // n for n in range(1, n_keys) if n_keys % (n * 128) == 0 and n_keys // n <= ATTN_MAX_CHUNK)
    nq_x, nq_c = s // tq, l // tq
    nq = nq_x + (nq_c if with_ctx_queries else 0)
    hw = 2 * hd
    k_col, v_col = a_w // hw, 2 * a_w // hw
    ctx_row = b * s // l

    def q_idx(bi, h, i):
        return (jnp.where(i < nq_x, bi * nq_x + i, b * nq_x + bi * nq_c + (i - nq_x)), h)

    def call(score_bound):
        return pl.pallas_call(
            functools.partial(_attn_kernel, hd=hd, tkv=tkv, tt=min(512, s), nq_x=nq_x,
                              with_ctx_queries=with_ctx_queries, score_bound=score_bound),
            out_shape=jax.ShapeDtypeStruct((t if with_ctx_queries else b * s, a_w), bf16),
            grid=(b, heads, nq),
            in_specs=[
                pl.BlockSpec((tq, hw), q_idx),
                pl.BlockSpec((s, hw), lambda bi, h, i: (bi, k_col + h)),
                pl.BlockSpec((l, hw), lambda bi, h, i: (ctx_row + bi, k_col + h)),
                pl.BlockSpec((s, hw), lambda bi, h, i: (bi, v_col + h)),
                pl.BlockSpec((l, hw), lambda bi, h, i: (ctx_row + bi, v_col + h)),
                pl.BlockSpec((1, 1), lambda bi, h, i: (0, 0)),
                pl.BlockSpec((1, 1), lambda bi, h, i: (0, 0)),
                pl.BlockSpec((1, hw), lambda bi, h, i: (0, 0)),
            ],
            out_specs=pl.BlockSpec((tq, hw), q_idx),
            scratch_shapes=[pltpu.VMEM((n_keys, hw), bf16), pltpu.VMEM((hw, n_keys), bf16)],
            compiler_params=_params("arbitrary", "arbitrary", "arbitrary"),
            name="diff_attn",
        )(qk2, qk2, qk2, parts, parts, lam, bound, out_gain)

    return lax.cond(bound[0, 0] <= ATTN_MAX_BOUND, lambda: call(True), lambda: call(False))


def _mprep_kernel(x_ref, p_ref, n_ref, w_ref, b_ref, sc_ref, o_ref, *, s, l, bs):
    tr = x_ref.shape[0]
    row0 = pl.program_id(0) * tr
    in_x = row0 < bs
    rel = jnp.where(in_x, row0 % s, (row0 - bs) % l)
    seq_len = jnp.where(in_x, s, l)
    keep_prev = jnp.where(rel == 0, 0.0, 1.0)
    keep_next = jnp.where(rel + tr == seq_len, 0.0, 1.0)
    x = x_ref[...].astype(f32)
    prev = p_ref[...].astype(f32)[BF16_ROWS - 1:BF16_ROWS] * keep_prev
    nxt = n_ref[...].astype(f32)[0:1] * keep_next
    rows = lax.broadcasted_iota(jnp.int32, x.shape, 0)
    xm = jnp.where(rows == 0, prev, pltpu.roll(x, 1, 0))
    xp = jnp.where(rows == tr - 1, nxt, pltpu.roll(x, tr - 1, 0))
    w = w_ref[...]
    u = b_ref[...] + xm * w[0:1] + x * w[1:2] + xp * w[2:3]
    o_ref[...] = (u * _sigmoid(u) * sc_ref[...]).astype(o_ref.dtype)


def _mlstm_prep(parts, col0, conv_w, conv_b, post, dims):
    b, s, l = dims
    t = parts.shape[0]
    width = conv_w.shape[1]
    tr = ROW_TILE
    wc = math.gcd(1024, math.gcd(width, col0))
    cb = col0 // wc
    sub = tr // BF16_ROWS
    last = t // BF16_ROWS - 1
    return pl.pallas_call(
        functools.partial(_mprep_kernel, s=s, l=l, bs=b * s),
        out_shape=jax.ShapeDtypeStruct((t, width), bf16),
        grid=(t // tr, width // wc),
        in_specs=[
            pl.BlockSpec((tr, wc), lambda i, j: (i, cb + j)),
            pl.BlockSpec((BF16_ROWS, wc), lambda i, j: (jnp.maximum(i * sub - 1, 0), cb + j)),
            pl.BlockSpec((BF16_ROWS, wc), lambda i, j: (jnp.minimum((i + 1) * sub, last), cb + j)),
            pl.BlockSpec((M_CONV, wc), lambda i, j: (0, j)),
            pl.BlockSpec((1, wc), lambda i, j: (0, j)),
            pl.BlockSpec((1, wc), lambda i, j: (0, j)),
        ],
        out_specs=pl.BlockSpec((tr, wc), lambda i, j: (i, j)),
        compiler_params=_params("arbitrary", "arbitrary"),
        name="mlstm_prep",
    )(parts, parts, parts, conv_w, conv_b, post)


def _scan_chunk(d, q, k, v, li_col, bs_col, li_row, bs_row, c_st, n_st, m_st):
    r = lax.broadcasted_iota(jnp.int32, (M_CHUNK, M_CHUNK), 0)
    c = lax.broadcasted_iota(jnp.int32, (M_CHUNK, M_CHUNK), 1)
    visible = (c <= r) if d == 0 else (c >= r)
    last = M_CHUNK - 1 if d == 0 else 0
    b_end = bs_row[:, last:last + 1]

    d_log = jnp.where(visible, bs_col + (li_row - bs_row), NEG)
    inter = bs_col + m_st
    m_pos = jnp.maximum(jnp.max(d_log, axis=1, keepdims=True), inter)
    w_intra = jnp.exp(d_log - m_pos)
    w_inter = jnp.exp(inter - m_pos)
    sc = lax.dot_general(q, k, (((1,), (1,)), ((), ())), preferred_element_type=f32) * w_intra
    num = jnp.dot(sc.astype(bf16), v, preferred_element_type=f32) + w_inter * jnp.dot(
        q, c_st.astype(bf16), preferred_element_type=f32)
    qn = jnp.sum(q.astype(f32) * n_st, axis=1, keepdims=True)
    den = jnp.sum(sc, axis=1, keepdims=True) + w_inter * qn
    hidden = num / jnp.maximum(jnp.abs(den), jnp.exp(-m_pos))

    w_log = b_end - bs_col + li_col
    m_new = jnp.maximum(b_end + m_st, jnp.max(w_log, axis=0, keepdims=True))
    decay = jnp.exp(b_end + m_st - m_new)
    wk = jnp.exp(w_log - m_new) * k.astype(f32)
    c_new = decay * c_st + lax.dot_general(wk.astype(bf16), v, (((0,), (0,)), ((), ())),
                                           preferred_element_type=f32)
    n_new = decay * n_st + jnp.sum(wk, axis=0, keepdims=True)
    return hidden, c_new, n_new, m_new


def _scan_kernel(*refs, mh, n_sub, n_heads):
    ins, outs, (c_ref, n_ref, m_ref) = refs[:10], refs[10:12], refs[12:]
    ng = 4 * mh
    dk, dv = c_ref.shape[1:]

    @pl.when(pl.program_id(2) == 0)
    def _():
        for ref in (c_ref, n_ref, m_ref):
            ref[...] = jnp.zeros(ref.shape, f32)

    chains = []
    for ci in range(2 * n_heads):
        hh, d = divmod(ci, 2)
        q_ref, k_ref, v_ref, gcol_ref, grow_ref = ins[5 * d:5 * d + 5]
        col_i = d * 2 * mh + pl.program_id(1) * n_heads + hh
        col_f = ng + col_i + mh
        gcol = gcol_ref[...]
        lane = lax.broadcasted_iota(jnp.int32, gcol.shape, 1)
        chains.append(dict(
            d=d, hh=hh, q=q_ref, k=k_ref, v=v_ref,
            li_col=jnp.sum(jnp.where(lane == col_i, gcol, 0.0), axis=1, keepdims=True),
            bs_col=jnp.sum(jnp.where(lane == col_f, gcol, 0.0), axis=1, keepdims=True),
            li_row=grow_ref[pl.ds(col_i, 1), :], bs_row=grow_ref[pl.ds(col_f, 1), :],
            state=(c_ref[ci], n_ref[ci], m_ref[ci]), hidden=[None] * n_sub))
    for j in range(n_sub):
        for ch in chains:
            sub = j if ch["d"] == 0 else n_sub - 1 - j
            rows = slice(sub * M_CHUNK, (sub + 1) * M_CHUNK)
            qk_cols = slice(ch["hh"] * dk, (ch["hh"] + 1) * dk)
            v_cols = slice(ch["hh"] * dv, (ch["hh"] + 1) * dv)
            ch["hidden"][sub], *ch["state"] = _scan_chunk(
                ch["d"], ch["q"][rows, qk_cols], ch["k"][rows, qk_cols], ch["v"][rows, v_cols],
                ch["li_col"][rows], ch["bs_col"][rows], ch["li_row"][:, rows], ch["bs_row"][:, rows], *ch["state"])
    for ch in chains:
        outs[ch["d"]][:, ch["hh"] * dv:(ch["hh"] + 1) * dv] = jnp.concatenate(ch["hidden"], axis=0).astype(bf16)
    for idx, ref in enumerate((c_ref, n_ref, m_ref)):
        ref[...] = jnp.stack([ch["state"][idx] for ch in chains])


def _mlstm_scan(mqk, parts, v_col0, gc_col, gc_row, dims):
    b, s, l, mh, dk, dv = dims
    t = mqk.shape[0]
    rows = SCAN_SUB * M_CHUNK
    assert l % rows == 0 and s % rows == 0
    n_lc, n_sc = l // rows, s // rows
    ctx0 = b * s // rows
    ng2 = gc_col.shape[1]

    def chunk(d, bi, st):
        in_ctx = st < n_lc
        local = jnp.where(in_ctx, st, st - n_lc)
        n_local = jnp.where(in_ctx, n_lc, n_sc)
        pos = local if d == 0 else n_local - 1 - local
        return jnp.where(in_ctx, ctx0 + bi * n_lc, bi * n_sc) + pos

    nh = math.gcd(SCAN_HEADS, mh)
    if v_col0 % (nh * dv):
        nh = 1
    vb = v_col0 // (nh * dv)

    def dir_specs(d):
        return [
            pl.BlockSpec((rows, nh * dk), lambda bi, g, st: (chunk(d, bi, st), g)),
            pl.BlockSpec((rows, nh * dk), lambda bi, g, st: (chunk(d, bi, st), mh // nh + g)),
            pl.BlockSpec((rows, nh * dv), lambda bi, g, st: (chunk(d, bi, st), vb + g)),
            pl.BlockSpec((rows, ng2), lambda bi, g, st: (chunk(d, bi, st), 0)),
            pl.BlockSpec((ng2, rows), lambda bi, g, st: (0, chunk(d, bi, st))),
        ]

    def out_spec(d):
        return pl.BlockSpec((rows, nh * dv), lambda bi, g, st: (chunk(d, bi, st), g))

    state = [pltpu.VMEM((2 * nh, dk, dv), f32), pltpu.VMEM((2 * nh, 1, dk), f32), pltpu.VMEM((2 * nh, 1, 1), f32)]
    args = (mqk, mqk, parts, gc_col, gc_row)
    return pl.pallas_call(
        functools.partial(_scan_kernel, mh=mh, n_sub=SCAN_SUB, n_heads=nh),
        out_shape=[jax.ShapeDtypeStruct((t, mh * dv), bf16)] * 2,
        grid=(b, mh // nh, n_lc + n_sc),
        in_specs=dir_specs(0) + dir_specs(1),
        out_specs=[out_spec(0), out_spec(1)],
        scratch_shapes=state,
        compiler_params=_params("arbitrary", "arbitrary", "arbitrary"),
        name="mlstm_scan",
    )(*args, *args)


def _mout_kernel(hf_ref, hb_ref, o_ref, g_ref, out_ref, *, dv):
    for hh in range(out_ref.shape[1] // dv):
        cols = slice(hh * dv, (hh + 1) * dv)
        hs = hf_ref[:, cols].astype(f32) + hb_ref[:, cols].astype(f32)
        ms = jnp.mean(hs * hs, axis=-1, keepdims=True)
        y = hs * lax.rsqrt(ms + EPS) * g_ref[:, cols]
        out_ref[:, cols] = (y * _sigmoid(o_ref[:, cols].astype(f32))).astype(out_ref.dtype)


def _mlstm_out(h_fwd, h_bwd, parts, o_col0, head_g, rows):
    m_w = h_fwd.shape[1]
    mh, dv = head_g.shape
    tr = ROW_TILE
    wb = math.gcd(MOUT_HEADS * dv, math.gcd(m_w, o_col0))
    ob = o_col0 // wb
    head_spec = pl.BlockSpec((tr, wb), lambda i, g: (i, g))
    return pl.pallas_call(
        functools.partial(_mout_kernel, dv=dv),
        out_shape=jax.ShapeDtypeStruct((rows, m_w), bf16),
        grid=(rows // tr, m_w // wb),
        in_specs=[head_spec, head_spec, pl.BlockSpec((tr, wb), lambda i, g: (i, ob + g)),
                  pl.BlockSpec((1, wb), lambda i, g: (0, g))],
        out_specs=head_spec,
        compiler_params=_params("arbitrary", "arbitrary"),
        name="mlstm_out",
    )(h_fwd, h_bwd, parts, head_g.reshape(1, m_w).astype(f32))


def _merge_kernel(a_ref, wa_ref, m_ref, wm_ref, ga_ref, gb_ref, o_ref):
    ya = jnp.dot(a_ref[...], wa_ref[...], preferred_element_type=f32)
    ym = jnp.dot(m_ref[...], wm_ref[...], preferred_element_type=f32)
    y = _sigmoid(ga_ref[...].astype(f32)) * ya + _sigmoid(gb_ref[...].astype(f32)) * ym
    o_ref[...] = y.astype(o_ref.dtype)


def _merge(a_flat, m_flat, w_a, w_m, layer, parts, ga_col0, rows):
    d = w_a.shape[2]
    tm, tn = 512, min(512, d)
    ga, gb = ga_col0 // tn, (ga_col0 + d) // tn
    return pl.pallas_call(
        _merge_kernel,
        out_shape=jax.ShapeDtypeStruct((rows, d), bf16),
        grid=(d // tn, rows // tm),
        in_specs=[
            pl.BlockSpec((tm, a_flat.shape[1]), lambda j, i: (i, 0)),
            pl.BlockSpec((None, w_a.shape[1], tn), lambda j, i: (layer, 0, j)),
            pl.BlockSpec((tm, m_flat.shape[1]), lambda j, i: (i, 0)),
            pl.BlockSpec((None, w_m.shape[1], tn), lambda j, i: (layer, 0, j)),
            pl.BlockSpec((tm, tn), lambda j, i: (i, ga + j)),
            pl.BlockSpec((tm, tn), lambda j, i: (i, gb + j)),
        ],
        out_specs=pl.BlockSpec((tm, tn), lambda j, i: (i, j)),
        compiler_params=_params("arbitrary", "arbitrary"),
        name="branch_merge",
    )(a_flat, w_a, m_flat, w_m, parts, parts)


def _outproj_kernel(y_ref, w_ref, h_ref, gate_ref, o_ref):
    o_ref[...] = h_ref[...] + gate_ref[0] * jnp.dot(y_ref[...], w_ref[...], preferred_element_type=f32)


def _out_proj(y, w_out, h, mods, layer, seg_of, rows):
    t, d = h.shape
    tm, tn = 512, min(1024, d)
    return pl.pallas_call(
        _outproj_kernel,
        out_shape=jax.ShapeDtypeStruct((t, d), f32),
        grid=(d // tn, rows // tm),
        in_specs=[
            pl.BlockSpec((tm, d), lambda j, i: (i, 0)),
            pl.BlockSpec((None, d, tn), lambda j, i: (layer, 0, j)),
            pl.BlockSpec((tm, tn), lambda j, i: (i, j)),
            pl.BlockSpec((1, 1, tn), lambda j, i: ((layer * 8 + seg_of(i, tm)) * 6 + 2, 0, j)),
        ],
        out_specs=pl.BlockSpec((tm, tn), lambda j, i: (i, j)),
        input_output_aliases={2: 0},
        compiler_params=_params("arbitrary", "arbitrary"),
        name="out_proj",
    )(y, w_out, h, mods)


def _route_kernel(h_ref, g_ref, shift_ref, scale_ref, wr_ref, bias_ref, n_ref, r_ref):
    h = h_ref[...]
    ms = jnp.mean(h * h, axis=-1, keepdims=True)
    n = h * lax.rsqrt(ms + EPS) * g_ref[...] * (1.0 + scale_ref[0]) + shift_ref[0]
    n_ref[...] = n
    logits = lax.dot_general(wr_ref[...], n, (((1,), (1,)), ((), ())), preferred_element_type=f32,
                             precision=lax.Precision.HIGHEST)
    score = _sigmoid(logits)
    sel = score + bias_ref[...]
    sel_e = [sel[e:e + 1] for e in range(N_GROUPS * GROUP_SIZE)]
    score_e = [score[e:e + 1] for e in range(N_GROUPS * GROUP_SIZE)]

    def top2_sum(v):
        best = v[0] + v[1]
        for a in range(GROUP_SIZE):
            for b in range(a + 1, GROUP_SIZE):
                if (a, b) != (0, 1):
                    best = jnp.maximum(best, v[a] + v[b])
        return best

    g_best = top2_sum(sel_e[:GROUP_SIZE])
    grp = jnp.zeros(g_best.shape, jnp.int32)
    for g in range(1, N_GROUPS):
        cand = top2_sum(sel_e[g * GROUP_SIZE:(g + 1) * GROUP_SIZE])
        better = cand > g_best
        grp = jnp.where(better, g, grp)
        g_best = jnp.where(better, cand, g_best)

    def in_group(rows):
        out = []
        for j in range(GROUP_SIZE):
            v = rows[j]
            for g in range(1, N_GROUPS):
                v = jnp.where(grp == g, rows[g * GROUP_SIZE + j], v)
            out.append(v)
        return out

    v, sc = in_group(sel_e), in_group(score_e)
    j1, b1, c1 = jnp.zeros(grp.shape, jnp.int32), v[0], sc[0]
    for j in range(1, GROUP_SIZE):
        better = v[j] > b1
        j1, b1, c1 = jnp.where(better, j, j1), jnp.where(better, v[j], b1), jnp.where(better, sc[j], c1)
    j2 = jnp.full(grp.shape, -1, jnp.int32)
    b2, c2 = jnp.full(b1.shape, -jnp.inf, f32), jnp.zeros(b1.shape, f32)
    for j in range(GROUP_SIZE):
        better = (j1 != j) & ((v[j] > b2) | (j2 < 0))
        j2, b2, c2 = jnp.where(better, j, j2), jnp.where(better, v[j], b2), jnp.where(better, sc[j], c2)
    tot = c1 + c2
    w1, w2 = c1 / tot, c2 / tot
    first_low = j1 < j2
    lo, hi = jnp.minimum(j1, j2), jnp.maximum(j1, j2)
    pair_in_group = jnp.zeros(lo.shape, jnp.int32)
    for idx, (p_lo, p_hi) in enumerate(PAIR_ORDER):
        pair_in_group = jnp.where((lo == p_lo) & (hi == p_hi), idx, pair_in_group)
    r_ref[...] = jnp.zeros(r_ref.shape, f32)
    r_ref[0:1, :] = (grp * 6 + pair_in_group).astype(f32)
    r_ref[1:2, :] = jnp.where(first_low, w1, w2)
    r_ref[2:3, :] = jnp.where(first_low, w2, w1)


def _norm_route(h, g, mods, layer, seg_of, w_router_t, bias, rows):
    t, d = h.shape
    e = w_router_t.shape[0]
    tr = ROW_TILE

    def mod_spec(k):
        return pl.BlockSpec((1, 1, d), lambda i: ((layer * 8 + seg_of(i, tr)) * 6 + k, 0, 0))

    return pl.pallas_call(
        _route_kernel,
        out_shape=[jax.ShapeDtypeStruct((rows, d), f32), jax.ShapeDtypeStruct((8, rows), f32)],
        grid=(rows // tr,),
        in_specs=[pl.BlockSpec((tr, d), lambda i: (i, 0)), pl.BlockSpec((1, d), lambda i: (0, 0)),
                  mod_spec(3), mod_spec(4), pl.BlockSpec((e, d), lambda i: (0, 0)),
                  pl.BlockSpec((e, 1), lambda i: (0, 0))],
        out_specs=[pl.BlockSpec((tr, d), lambda i: (i, 0)), pl.BlockSpec((8, tr), lambda i: (0, i))],
        compiler_params=_params("arbitrary"),
        name="norm_route",
    )(h, g.reshape(1, d), mods, mods, w_router_t, bias.reshape(e, 1).astype(f32))


def _moe_kernel(lo_ref, hi_ref, nv_ref, tok_ref, x_hbm, w_ref, wg_lo, wu_lo, wd_lo, wg_hi, wu_hi, wd_hi,
                out_hbm, xbuf, ybuf, sem_in, sem_out):
    blk = pl.program_id(0)
    slot = blk % 2

    def row_in(b, r):
        return pltpu.make_async_copy(x_hbm.at[pl.ds(tok_ref[b * MOE_BLOCK + r], 1)],
                                     xbuf.at[b % 2, pl.ds(r, 1)], sem_in.at[b % 2])

    def row_out(b, r):
        return pltpu.make_async_copy(ybuf.at[b % 2, pl.ds(r, 1)],
                                     out_hbm.at[pl.ds(tok_ref[b * MOE_BLOCK + r], 1)], sem_out.at[b % 2])

    def for_rows(fn):
        def body(r, c):
            fn(r)
            return c

        lax.fori_loop(0, MOE_BLOCK, body, 0, unroll=8)

    def start_in(b):
        for r in range(MOE_BLOCK):
            row_in(b, r).start()

    def start_out(b):
        nv = nv_ref[b]

        @pl.when(nv == MOE_BLOCK)
        def _():
            for r in range(MOE_BLOCK):
                row_out(b, r).start()

        @pl.when(nv < MOE_BLOCK)
        def _():
            for_rows(lambda r: pl.when(r < nv)(lambda: row_out(b, r).start()))

    def wait_in(b):
        pltpu.make_async_copy(x_hbm.at[pl.ds(0, MOE_BLOCK)], xbuf.at[b % 2], sem_in.at[b % 2]).wait()

    def wait_out(b):
        nv = nv_ref[b]
        p = 1
        while p <= MOE_BLOCK:
            rows = pl.ds(0, p)
            pl.when((nv & p) != 0)(
                pltpu.make_async_copy(ybuf.at[b % 2, rows], out_hbm.at[rows], sem_out.at[b % 2]).wait)
            p *= 2

    def has_rows(b):
        n_blk = pl.num_programs(0)
        return jnp.logical_and(b < n_blk, nv_ref[jnp.minimum(b, n_blk - 1)] > 0)

    @pl.when(nv_ref[blk] > 0)
    def _():
        @pl.when(blk == 0)
        def _():
            start_in(blk)

        wait_in(blk)

        @pl.when(has_rows(blk + 1))
        def _():
            start_in(blk + 1)

        x = xbuf[slot].astype(bf16)
        eye = (lax.broadcasted_iota(jnp.int32, (MOE_BLOCK, MOE_BLOCK), 0)
               == lax.broadcasted_iota(jnp.int32, (MOE_BLOCK, MOE_BLOCK), 1))
        w = w_ref[0]

        def expert(wg, wu, wd, w_row):
            a = jnp.dot(x, wg[0], preferred_element_type=f32)
            u = jnp.dot(x, wu[0], preferred_element_type=f32)
            y = jnp.dot((a * _sigmoid(a) * u).astype(bf16), wd[0], preferred_element_type=f32)
            return y * jnp.sum(jnp.where(eye, w_row, 0.0), axis=1, keepdims=True)

        ybuf[slot] = expert(wg_lo, wu_lo, wd_lo, w[0:1]) + expert(wg_hi, wu_hi, wd_hi, w[1:2])

        @pl.when(blk > 0)
        def _():
            wait_out(blk - 1)

        start_out(blk)

        @pl.when(jnp.logical_not(has_rows(blk + 1)))
        def _():
            wait_out(blk)


def _moe(n2, route, w_gate, w_up, w_down, layer, n_tok):
    d = n2.shape[1]
    dff = w_gate.shape[-1]
    blk = MOE_BLOCK
    n_blk = -(-n_tok // blk) + N_PAIRS
    n_pad = n_blk * blk
    pair = route[0, :n_tok].astype(jnp.int32)
    onehot = (pair[:, None] == jnp.arange(N_PAIRS, dtype=jnp.int32)[None, :]).astype(jnp.int32)
    csum = jnp.cumsum(onehot, axis=0)
    rank = jnp.sum(csum * onehot, axis=1) - 1
    counts = csum[-1]
    padded = (counts + blk - 1) // blk * blk
    pad_end = jnp.cumsum(padded)
    pad_start = pad_end - padded
    dest = pad_start[pair] + rank
    buf_tok = jnp.zeros((n_pad,), jnp.int32).at[dest].set(jnp.arange(n_tok, dtype=jnp.int32))
    wts = jnp.zeros((2, n_pad), f32).at[:, dest].set(route[1:3, :n_tok])
    wts = wts.reshape(2, n_blk, blk).transpose(1, 0, 2)
    blk_start = jnp.arange(n_blk, dtype=jnp.int32) * blk
    seg = jnp.minimum(jnp.searchsorted(pad_end, blk_start, side="right"), N_PAIRS - 1).astype(jnp.int32)
    nv = jnp.clip(counts[seg] - (blk_start - pad_start[seg]), 0, blk).astype(jnp.int32)
    pair_tbl = jnp.asarray(PAIR_ORDER, jnp.int32)
    e_lo = (seg // len(PAIR_ORDER)) * GROUP_SIZE + pair_tbl[seg % len(PAIR_ORDER), 0]
    e_hi = (seg // len(PAIR_ORDER)) * GROUP_SIZE + pair_tbl[seg % len(PAIR_ORDER), 1]

    def w_in(sel):
        return pl.BlockSpec((None, 1, d, dff), lambda i, lo, hi, nvr, tok: (layer, (lo, hi)[sel][i], 0, 0),
                            pipeline_mode=pl.Buffered(1))

    def w_out(sel):
        return pl.BlockSpec((None, 1, dff, d), lambda i, lo, hi, nvr, tok: (layer, (lo, hi)[sel][i], 0, 0),
                            pipeline_mode=pl.Buffered(1))

    return pl.pallas_call(
        _moe_kernel,
        out_shape=jax.ShapeDtypeStruct((n_tok, d), f32),
        grid_spec=pltpu.PrefetchScalarGridSpec(
            num_scalar_prefetch=4,
            grid=(n_blk,),
            in_specs=[pl.BlockSpec(memory_space=pl.ANY),
                      pl.BlockSpec((1, 2, blk), lambda i, lo, hi, nvr, tok: (i, 0, 0)),
                      w_in(0), w_in(0), w_out(0), w_in(1), w_in(1), w_out(1)],
            out_specs=pl.BlockSpec(memory_space=pl.ANY),
            scratch_shapes=[pltpu.VMEM((2, blk, d), f32), pltpu.VMEM((2, blk, d), f32),
                            pltpu.SemaphoreType.DMA((2,)), pltpu.SemaphoreType.DMA((2,))],
        ),
        compiler_params=_params("arbitrary"),
        name="moe",
    )(e_lo, e_hi, nv, buf_tok, n2, wts, w_gate, w_up, w_down, w_gate, w_up, w_down)


def _rope_tables(n_rows, hd, pad_rows):
    n_freq = hd // 4
    inv_freq = ROPE_THETA ** (-jnp.arange(n_freq, dtype=f32) / n_freq)
    pos = jnp.arange(n_rows * GRID_W)
    ang_row = (pos // GRID_W).astype(f32)[:, None] * inv_freq
    ang_col = (pos % GRID_W).astype(f32)[:, None] * inv_freq
    ang = jnp.concatenate([ang_row, ang_row, ang_col, ang_col], axis=1)
    sign = jnp.tile(jnp.concatenate([-jnp.ones((n_freq,), f32), jnp.ones((n_freq,), f32)]), 2)
    cos_t = jnp.concatenate([jnp.cos(ang), jnp.ones((pad_rows, hd), f32)], axis=0)
    sin_t = jnp.concatenate([jnp.sin(ang) * sign, jnp.zeros((pad_rows, hd), f32)], axis=0)
    return cos_t, sin_t


def kernel(x, c, ctx, c_ctx, w_ada, b_ada, g_norm1, g_norm2, w_in, q_norm_g, k_norm_g, diff_lambda, attn_head_g,
           conv_w, conv_b, mlstm_gate_b, mlstm_head_g, w_br_attn, w_br_mlstm, w_out, w_router, router_bias,
           w_gate, w_up, w_down):
    b, s, d = x.shape
    l = ctx.shape[1]
    depth = w_ada.shape[0]
    hd = q_norm_g.shape[-1]
    a_w = w_br_attn.shape[1]
    heads = a_w // (2 * hd)
    mh, dv = mlstm_head_g.shape[-2:]
    qk_w = conv_w.shape[-1] // 2
    dk = qk_w // mh
    m_w = mh * dv
    ng = 4 * mh
    assert w_router.shape[1] == N_GROUPS * GROUP_SIZE
    assert s % ROW_TILE == 0 and l % ROW_TILE == 0 and s % GRID_W == 0 and b + 1 <= 8
    t_x, t_all = b * s, b * s + b * l
    gate_col = 3 * a_w + 2 * qk_w + 2 * m_w
    mq_col, mv_col, mo_col = 3 * a_w, 3 * a_w + 2 * qk_w, 3 * a_w + 2 * qk_w + m_w

    def seg_of(i, tile):
        return jnp.where(i < t_x // tile, i // (s // tile), b)

    w_in_bf = w_in.astype(bf16)
    w_g = w_in_bf[:, :, gate_col:gate_col + ng]
    w_bg = w_in_bf[:, :, gate_col + ng:]
    w_a, w_m, w_o = w_br_attn.astype(bf16), w_br_mlstm.astype(bf16), w_out.astype(bf16)
    w_eg, w_eu, w_ed = w_gate.astype(bf16), w_up.astype(bf16), w_down.astype(bf16)
    cv = jnp.zeros((8, d), f32).at[:b].set(c).at[b].set(c_ctx)
    mods = _ada(cv, b + 1, w_ada, b_ada).reshape(depth * 8 * 6, 1, d)
    cos_t, sin_t = _rope_tables(s // GRID_W, hd, ROW_TILE)
    h = jnp.concatenate([x.reshape(t_x, d), ctx.reshape(b * l, d)], axis=0)
    f = None
    for layer in range(depth):
        need_ctx = layer < depth - 1
        rows = t_all if need_ctx else t_x
        lam_init = 0.8 - 0.6 * math.exp(-0.3 * layer)
        lv = diff_lambda[layer].astype(f32)
        lam = (jnp.exp(jnp.sum(lv[0] * lv[1])) - jnp.exp(jnp.sum(lv[2] * lv[3])) + lam_init).reshape(1, 1)
        if f is None:
            (n1,) = _resid_norm(h, t_all, seg_of, mods, norm=(g_norm1[layer], layer, 0, 1))
        else:
            h, n1 = _resid_norm(h, t_all, seg_of, mods, resid=(f, layer - 1, 5), norm=(g_norm1[layer], layer, 0, 1))
        parts = _matmul(n1, w_in_bf, layer, bf16, n=gate_col)
        branch_gates = _matmul(n1, w_bg, layer, bf16)
        g_col, c_col = _gates(n1, w_g, layer, mlstm_gate_b[layer], mh)

        qk_gain = jnp.concatenate([jnp.tile(q_norm_g[layer].astype(f32), 2 * heads) * hd ** -0.5,
                                   jnp.tile(k_norm_g[layer].astype(f32), 2 * heads)]).reshape(1, 2 * a_w)
        qk2 = _qk_prep(parts, qk_gain, cos_t, sin_t, 2 * a_w, t_x // ROW_TILE, s // ROW_TILE, hd)
        out_gain = (attn_head_g[layer].astype(f32) * (1.0 - lam_init)).reshape(1, 2 * hd)
        bound = (1.01 * hd ** 0.5 * jnp.max(jnp.abs(q_norm_g[layer])) * jnp.max(jnp.abs(k_norm_g[layer])))
        a_flat = _attention(qk2, parts, lam, bound.astype(f32).reshape(1, 1), out_gain, (b, s, l, heads, hd), need_ctx)

        post = jnp.concatenate([jnp.ones((qk_w,), f32), jnp.full((qk_w,), dk ** -0.5, f32)]).reshape(1, 2 * qk_w)
        mqk = _mlstm_prep(parts, mq_col, conv_w[layer].astype(f32), conv_b[layer].reshape(1, -1).astype(f32), post,
                          (b, s, l))
        gc_col = jnp.concatenate([g_col, c_col], axis=1)
        h_fwd, h_bwd = _mlstm_scan(mqk, parts, mv_col, gc_col, gc_col.T, (b, s, l, mh, dk, dv))
        m_flat = _mlstm_out(h_fwd, h_bwd, parts, mo_col, mlstm_head_g[layer], rows)

        y = _merge(a_flat, m_flat, w_a, w_m, layer, branch_gates, 0, rows)
        h = _out_proj(y, w_o, h, mods, layer, seg_of, rows)
        n2, route = _norm_route(h, g_norm2[layer], mods, layer, seg_of, w_router.T.astype(f32), router_bias, rows)
        f = _moe(n2, route, w_eg, w_eu, w_ed, layer, rows)
    (h,) = _resid_norm(h, t_x, seg_of, mods, resid=(f, depth - 1, 5))
    return h.reshape(b, s, d)
```

```python
import functools
import math

import jax
import jax.numpy as jnp
from jax import lax
from jax.experimental import pallas as pl
from jax.experimental.pallas import tpu as pltpu

GRID_W = 64
EPS = 1e-6
ROPE_THETA = 10000.0
M_CHUNK = 128
M_CONV = 3
GATE_CAP = 15.0
N_GROUPS = 4
GROUP_SIZE = 4
PAIR_ORDER = ((0, 1), (0, 2), (1, 2), (1, 3), (0, 3), (2, 3))
N_PAIRS = N_GROUPS * len(PAIR_ORDER)
NEG = -1e30

V7X_VMEM_LIMIT = 56 * 1024 * 1024
ROW_TILE = 256
MOE_BLOCK = 128
BF16_ROWS = 16
SCAN_SUB = 2
SCAN_HEADS = 2
MOUT_HEADS = 2
ATTN_MAX_CHUNK = 3072
ATTN_MAX_BOUND = 40.0

f32 = jnp.float32
bf16 = jnp.bfloat16


def _params(*sem):
    return pltpu.CompilerParams(dimension_semantics=sem, vmem_limit_bytes=V7X_VMEM_LIMIT)


def _sigmoid(x):
    return 1.0 / (1.0 + jnp.exp(-x))


def _ada_kernel(cvt_ref, w_ref, b_ref, o_ref, *, n_rows):
    cvt = cvt_ref[...]
    s = cvt * _sigmoid(cvt)
    w = w_ref[0]
    o_ref[0] = jnp.zeros(o_ref.shape[1:], f32)
    for r in range(n_rows):
        o_ref[0, r:r + 1, :] = jnp.sum(w * s[:, r:r + 1], axis=0, keepdims=True) + b_ref[0]


def _ada(cv, n_rows, w_ada, b_ada):
    depth, d, n6 = w_ada.shape
    tn = min(512, n6)
    return pl.pallas_call(
        functools.partial(_ada_kernel, n_rows=n_rows),
        out_shape=jax.ShapeDtypeStruct((depth, 8, n6), f32),
        grid=(depth, n6 // tn),
        in_specs=[
            pl.BlockSpec((d, 8), lambda l, j: (0, 0)),
            pl.BlockSpec((1, d, tn), lambda l, j: (l, 0, j)),
            pl.BlockSpec((1, 1, tn), lambda l, j: (l, 0, j)),
        ],
        out_specs=pl.BlockSpec((1, 8, tn), lambda l, j: (l, 0, j)),
        compiler_params=_params("arbitrary", "arbitrary"),
        name="ada",
    )(cv.T, w_ada, b_ada.reshape(depth, 1, n6))


def _norm_kernel(*refs, has_resid, has_norm):
    it = iter(refs)
    h_ref = next(it)
    if has_resid:
        f_ref, gate_ref = next(it), next(it)
    if has_norm:
        g_ref, shift_ref, scale_ref = next(it), next(it), next(it)
    h = h_ref[...]
    if has_resid:
        h = h + gate_ref[0] * f_ref[...]
        next(it)[...] = h
    if has_norm:
        ms = jnp.mean(h * h, axis=-1, keepdims=True)
        y = h * lax.rsqrt(ms + EPS) * g_ref[...]
        n_ref = next(it)
        n_ref[...] = (y * (1.0 + scale_ref[0]) + shift_ref[0]).astype(n_ref.dtype)


def _resid_norm(h, rows, seg_of, mods, resid=None, norm=None, n_dtype=bf16):
    t, d = h.shape
    tr = ROW_TILE
    row_spec = pl.BlockSpec((tr, d), lambda i: (i, 0))

    def mod_spec(layer, k):
        return pl.BlockSpec((1, 1, d), lambda i: ((layer * 8 + seg_of(i, tr)) * 6 + k, 0, 0))

    args, in_specs, out_shape, out_specs, aliases = [h], [row_spec], [], [], {}
    if resid is not None:
        f, layer, k_gate = resid
        args += [f, mods]
        in_specs += [row_spec, mod_spec(layer, k_gate)]
        out_shape.append(jax.ShapeDtypeStruct((t if norm is not None else rows, d), f32))
        out_specs.append(row_spec)
        aliases = {0: 0} if norm is not None else {}
    if norm is not None:
        g, layer, k_shift, k_scale = norm
        args += [g.reshape(1, d), mods, mods]
        in_specs += [pl.BlockSpec((1, d), lambda i: (0, 0)), mod_spec(layer, k_shift), mod_spec(layer, k_scale)]
        out_shape.append(jax.ShapeDtypeStruct((t, d), n_dtype))
        out_specs.append(row_spec)
    outs = pl.pallas_call(
        functools.partial(_norm_kernel, has_resid=resid is not None, has_norm=norm is not None),
        out_shape=out_shape,
        grid=(rows // tr,),
        in_specs=in_specs,
        out_specs=out_specs,
        input_output_aliases=aliases,
        compiler_params=_params("arbitrary"),
        name="resid_norm",
    )(*args)
    return outs


def _mm_kernel(a_ref, b_ref, o_ref):
    o_ref[...] = jnp.dot(a_ref[...], b_ref[...], preferred_element_type=f32).astype(o_ref.dtype)


def _matmul(a, b, layer, out_dtype, n=None, tm=512, tn=1024):
    m, k = a.shape
    n = n or b.shape[2]
    tm, tn = min(tm, m), math.gcd(tn, n)
    assert m % tm == 0 and tn % 128 == 0
    return pl.pallas_call(
        _mm_kernel,
        out_shape=jax.ShapeDtypeStruct((m, n), out_dtype),
        grid=(n // tn, m // tm),
        in_specs=[pl.BlockSpec((tm, k), lambda j, i: (i, 0)), pl.BlockSpec((None, k, tn), lambda j, i: (layer, 0, j))],
        out_specs=pl.BlockSpec((tm, tn), lambda j, i: (i, j)),
        compiler_params=_params("arbitrary", "arbitrary"),
        name="in_proj",
    )(a, b)


def _gates_kernel(n_ref, w_ref, b_ref, g_ref, c_ref, *, mh):
    tr = n_ref.shape[0]
    mg = jnp.dot(n_ref[...], w_ref[...], preferred_element_type=f32) + b_ref[...]
    g = GATE_CAP * jnp.tanh(mg * (1.0 / GATE_CAP))
    lane = lax.broadcasted_iota(jnp.int32, g.shape, 1)
    is_forget = (lane // mh) % 2 == 1
    log_sig = jnp.minimum(g, 0.0) - jnp.log(1.0 + jnp.exp(-jnp.abs(g)))
    val = jnp.where(is_forget, log_sig, g)
    g_ref[...] = val
    r = lax.broadcasted_iota(jnp.int32, (M_CHUNK, M_CHUNK), 0)
    c = lax.broadcasted_iota(jnp.int32, (M_CHUNK, M_CHUNK), 1)
    lower = (c <= r).astype(f32)
    upper = (c >= r).astype(f32)
    fwd = lax.broadcasted_iota(jnp.int32, (M_CHUNK, g.shape[1]), 1) < 2 * mh
    for ch in range(tr // M_CHUNK):
        v = val[ch * M_CHUNK:(ch + 1) * M_CHUNK]
        cf = jnp.dot(lower, v, preferred_element_type=f32, precision=lax.Precision.HIGHEST)
        cb = jnp.dot(upper, v, preferred_element_type=f32, precision=lax.Precision.HIGHEST)
        c_ref[ch * M_CHUNK:(ch + 1) * M_CHUNK, :] = jnp.where(fwd, cf, cb)


def _gates(n, w_g, layer, gate_b, mh):
    t, d = n.shape
    ng = 4 * mh
    tr = ROW_TILE
    spec = pl.BlockSpec((tr, ng), lambda i: (i, 0))
    return pl.pallas_call(
        functools.partial(_gates_kernel, mh=mh),
        out_shape=[jax.ShapeDtypeStruct((t, ng), f32)] * 2,
        grid=(t // tr,),
        in_specs=[pl.BlockSpec((tr, d), lambda i: (i, 0)), pl.BlockSpec((None, d, ng), lambda i: (layer, 0, 0)),
                  pl.BlockSpec((1, ng), lambda i: (0, 0))],
        out_specs=[spec, spec],
        compiler_params=_params("arbitrary"),
        name="gates",
    )(n, w_g, gate_b.reshape(1, ng).astype(f32))


def _qk_kernel(x_ref, g_ref, cos_ref, sin_ref, o_ref, *, hd):
    x = x_ref[...].astype(f32)
    cos, sin = cos_ref[...], sin_ref[...]
    r = lax.broadcasted_iota(jnp.int32, (hd, hd), 0)
    c = lax.broadcasted_iota(jnp.int32, (hd, hd), 1)
    swap = jnp.where(r == (c ^ (hd // 4)), 1.0, 0.0).astype(bf16)
    ones = jnp.ones((hd, hd), bf16)
    for s in range(x.shape[1] // hd):
        sl = slice(s * hd, (s + 1) * hd)
        xs = x[:, sl]
        ssq = jnp.dot((xs * xs).astype(bf16), ones, preferred_element_type=f32)
        y = xs * lax.rsqrt(ssq * (1.0 / hd) + EPS) * g_ref[:, sl]
        partner = jnp.dot(y.astype(bf16), swap, preferred_element_type=f32)
        o_ref[:, sl] = (y * cos + partner * sin).astype(o_ref.dtype)


def _qk_prep(parts, gain, cos_t, sin_t, width, n_x_tiles, tiles_per_seq, hd):
    t = parts.shape[0]
    tr = ROW_TILE
    wq = min(1024, width)

    def tbl(i, j):
        return (jnp.where(i < n_x_tiles, i % tiles_per_seq, tiles_per_seq), 0)

    return pl.pallas_call(
        functools.partial(_qk_kernel, hd=hd),
        out_shape=jax.ShapeDtypeStruct((t, width), bf16),
        grid=(t // tr, width // wq),
        in_specs=[pl.BlockSpec((tr, wq), lambda i, j: (i, j)), pl.BlockSpec((1, wq), lambda i, j: (0, j)),
                  pl.BlockSpec((tr, hd), tbl), pl.BlockSpec((tr, hd), tbl)],
        out_specs=pl.BlockSpec((tr, wq), lambda i, j: (i, j)),
        compiler_params=_params("arbitrary", "arbitrary"),
        name="qk_prep",
    )(parts, gain, cos_t, sin_t)


def _attn_kernel(q_ref, kx_ref, kc_ref, vx_ref, vc_ref, lam_ref, bound_ref, g_ref, o_ref, k_ref, vt_ref, *, hd, tkv,
                 tt, nq_x, with_ctx_queries, score_bound):
    i = pl.program_id(2)
    tq = q_ref.shape[0]
    l_ctx, s = kc_ref.shape[0], kx_ref.shape[0]

    @pl.when(i == 0)
    def _():
        k_ref[0:l_ctx] = kc_ref[...]
        vt_ref[:, 0:l_ctx] = vc_ref[...].T
        for c in range(s // tt):
            k_ref[l_ctx + c * tt:l_ctx + (c + 1) * tt] = kx_ref[c * tt:(c + 1) * tt]
            vt_ref[:, l_ctx + c * tt:l_ctx + (c + 1) * tt] = vx_ref[c * tt:(c + 1) * tt].T

    def attend(n_keys, tk):
        qt = q_ref[...].T
        qt0, qt1 = qt[:hd], qt[hd:]

        def scores(c):
            k = k_ref[c * tk:(c + 1) * tk]
            return jnp.concatenate([jnp.dot(k[:, :hd], qt0, preferred_element_type=f32),
                                    jnp.dot(k[:, hd:], qt1, preferred_element_type=f32)], axis=1)

        m = l_sum = acc = None
        st = scores(0)
        for c in range(n_keys // tk):
            st_next = scores(c + 1) if (c + 1) * tk < n_keys else None
            if score_bound:
                m_new = bound_ref[...]
            else:
                m_c = jnp.max(st, axis=0, keepdims=True)
                m_new = m_c if m is None else jnp.maximum(m, m_c)
            p = jnp.exp(st - m_new)
            pv = jnp.dot(vt_ref[:, c * tk:(c + 1) * tk], p.astype(bf16), preferred_element_type=f32)
            if m is None:
                l_sum, acc = jnp.sum(p, axis=0, keepdims=True), pv
            elif score_bound:
                l_sum, acc = l_sum + jnp.sum(p, axis=0, keepdims=True), acc + pv
            else:
                alpha = jnp.exp(m - m_new)
                l_sum, acc = alpha * l_sum + jnp.sum(p, axis=0, keepdims=True), alpha * acc + pv
            m, st = m_new, st_next
        acc = acc / l_sum
        a = (acc[:, :tq] - lam_ref[...] * acc[:, tq:]).T
        ms = jnp.mean(a * a, axis=-1, keepdims=True)
        o_ref[...] = (a * lax.rsqrt(ms + EPS) * g_ref[...]).astype(o_ref.dtype)

    if with_ctx_queries:
        pl.when(i < nq_x)(lambda: attend(l_ctx + s, tkv))
        pl.when(i >= nq_x)(lambda: attend(l_ctx, l_ctx))
    else:
        attend(l_ctx + s, tkv)


def _attention(qk2, parts, lam, bound, out_gain, dims, with_ctx_queries):
    b, s, l, heads, hd = dims
    a_w = heads * 2 * hd
    t = qk2.shape[0]
    tq = ROW_TILE
    n_keys = s + l
    tkv = next(n_keys // n for n in range(1, n_keys) if n_keys % (n * 128) == 0 and n_keys // n <= ATTN_MAX_CHUNK)
    nq_x, nq_c = s // tq, l // tq
    nq = nq_x + (nq_c if with_ctx_queries else 0)
    hw = 2 * hd
    k_col, v_col = a_w // hw, 2 * a_w // hw
    ctx_row = b * s // l

    def q_idx(bi, h, i):
        return (jnp.where(i < nq_x, bi * nq_x + i, b * nq_x + bi * nq_c + (i - nq_x)), h)

    def call(score_bound):
        return pl.pallas_call(
            functools.partial(_attn_kernel, hd=hd, tkv=tkv, tt=min(512, s), nq_x=nq_x,
                              with_ctx_queries=with_ctx_queries, score_bound=score_bound),
            out_shape=jax.ShapeDtypeStruct((t if with_ctx_queries else b * s, a_w), bf16),
            grid=(b, heads, nq),
            in_specs=[
                pl.BlockSpec((tq, hw), q_idx),
                pl.BlockSpec((s, hw), lambda bi, h, i: (bi, k_col + h)),
                pl.BlockSpec((l, hw), lambda bi, h, i: (ctx_row + bi, k_col + h)),
                pl.BlockSpec((s, hw), lambda bi, h, i: (bi, v_col + h)),
                pl.BlockSpec((l, hw), lambda bi, h, i: (ctx_row + bi, v_col + h)),
                pl.BlockSpec((1, 1), lambda bi, h, i: (0, 0)),
                pl.BlockSpec((1, 1), lambda bi, h, i: (0, 0)),
                pl.BlockSpec((1, hw), lambda bi, h, i: (0, 0)),
            ],
            out_specs=pl.BlockSpec((tq, hw), q_idx),
            scratch_shapes=[pltpu.VMEM((n_keys, hw), bf16), pltpu.VMEM((hw, n_keys), bf16)],
            compiler_params=_params("arbitrary", "arbitrary", "arbitrary"),
            name="diff_attn",
        )(qk2, qk2, qk2, parts, parts, lam, bound, out_gain)

    return lax.cond(bound[0, 0] <= ATTN_MAX_BOUND, lambda: call(True), lambda: call(False))


def _mprep_kernel(x_ref, p_ref, n_ref, w_ref, b_ref, sc_ref, o_ref, *, s, l, bs):
    tr = x_ref.shape[0]
    row0 = pl.program_id(0) * tr
    in_x = row0 < bs
    rel = jnp.where(in_x, row0 % s, (row0 - bs) % l)
    seq_len = jnp.where(in_x, s, l)
    keep_prev = jnp.where(rel == 0, 0.0, 1.0)
    keep_next = jnp.where(rel + tr == seq_len, 0.0, 1.0)
    x = x_ref[...].astype(f32)
    prev = p_ref[...].astype(f32)[BF16_ROWS - 1:BF16_ROWS] * keep_prev
    nxt = n_ref[...].astype(f32)[0:1] * keep_next
    rows = lax.broadcasted_iota(jnp.int32, x.shape, 0)
    xm = jnp.where(rows == 0, prev, pltpu.roll(x, 1, 0))
    xp = jnp.where(rows == tr - 1, nxt, pltpu.roll(x, tr - 1, 0))
    w = w_ref[...]
    u = b_ref[...] + xm * w[0:1] + x * w[1:2] + xp * w[2:3]
    o_ref[...] = (u * _sigmoid(u) * sc_ref[...]).astype(o_ref.dtype)


def _mlstm_prep(parts, col0, conv_w, conv_b, post, dims):
    b, s, l = dims
    t = parts.shape[0]
    width = conv_w.shape[1]
    tr = ROW_TILE
    wc = math.gcd(1024, math.gcd(width, col0))
    cb = col0 // wc
    sub = tr // BF16_ROWS
    last = t // BF16_ROWS - 1
    return pl.pallas_call(
        functools.partial(_mprep_kernel, s=s, l=l, bs=b * s),
        out_shape=jax.ShapeDtypeStruct((t, width), bf16),
        grid=(t // tr, width // wc),
        in_specs=[
            pl.BlockSpec((tr, wc), lambda i, j: (i, cb + j)),
            pl.BlockSpec((BF16_ROWS, wc), lambda i, j: (jnp.maximum(i * sub - 1, 0), cb + j)),
            pl.BlockSpec((BF16_ROWS, wc), lambda i, j: (jnp.minimum((i + 1) * sub, last), cb + j)),
            pl.BlockSpec((M_CONV, wc), lambda i, j: (0, j)),
            pl.BlockSpec((1, wc), lambda i, j: (0, j)),
            pl.BlockSpec((1, wc), lambda i, j: (0, j)),
        ],
        out_specs=pl.BlockSpec((tr, wc), lambda i, j: (i, j)),
        compiler_params=_params("arbitrary", "arbitrary"),
        name="mlstm_prep",
    )(parts, parts, parts, conv_w, conv_b, post)


def _scan_chunk(d, q, k, v, li_col, bs_col, li_row, bs_row, c_st, n_st, m_st):
    r = lax.broadcasted_iota(jnp.int32, (M_CHUNK, M_CHUNK), 0)
    c = lax.broadcasted_iota(jnp.int32, (M_CHUNK, M_CHUNK), 1)
    visible = (c <= r) if d == 0 else (c >= r)
    last = M_CHUNK - 1 if d == 0 else 0
    b_end = bs_row[:, last:last + 1]

    d_log = jnp.where(visible, bs_col + (li_row - bs_row), NEG)
    inter = bs_col + m_st
    m_pos = jnp.maximum(jnp.max(d_log, axis=1, keepdims=True), inter)
    w_intra = jnp.exp(d_log - m_pos)
    w_inter = jnp.exp(inter - m_pos)
    sc = lax.dot_general(q, k, (((1,), (1,)), ((), ())), preferred_element_type=f32) * w_intra
    num = jnp.dot(sc.astype(bf16), v, preferred_element_type=f32) + w_inter * jnp.dot(
        q, c_st.astype(bf16), preferred_element_type=f32)
    qn = jnp.sum(q.astype(f32) * n_st, axis=1, keepdims=True)
    den = jnp.sum(sc, axis=1, keepdims=True) + w_inter * qn
    hidden = num / jnp.maximum(jnp.abs(den), jnp.exp(-m_pos))

    w_log = b_end - bs_col + li_col
    m_new = jnp.maximum(b_end + m_st, jnp.max(w_log, axis=0, keepdims=True))
    decay = jnp.exp(b_end + m_st - m_new)
    wk = jnp.exp(w_log - m_new) * k.astype(f32)
    c_new = decay * c_st + lax.dot_general(wk.astype(bf16), v, (((0,), (0,)), ((), ())),
                                           preferred_element_type=f32)
    n_new = decay * n_st + jnp.sum(wk, axis=0, keepdims=True)
    return hidden, c_new, n_new, m_new


def _scan_kernel(*refs, mh, n_sub, n_heads):
    ins, outs, (c_ref, n_ref, m_ref) = refs[:10], refs[10:12], refs[12:]
    ng = 4 * mh
    dk, dv = c_ref.shape[1:]

    @pl.when(pl.program_id(2) == 0)
    def _():
        for ref in (c_ref, n_ref, m_ref):
            ref[...] = jnp.zeros(ref.shape, f32)

    chains = []
    for ci in range(2 * n_heads):
        hh, d = divmod(ci, 2)
        q_ref, k_ref, v_ref, gcol_ref, grow_ref = ins[5 * d:5 * d + 5]
        col_i = d * 2 * mh + pl.program_id(1) * n_heads + hh
        col_f = ng + col_i + mh
        gcol = gcol_ref[...]
        lane = lax.broadcasted_iota(jnp.int32, gcol.shape, 1)
        chains.append(dict(
            d=d, hh=hh, q=q_ref, k=k_ref, v=v_ref,
            li_col=jnp.sum(jnp.where(lane == col_i, gcol, 0.0), axis=1, keepdims=True),
            bs_col=jnp.sum(jnp.where(lane == col_f, gcol, 0.0), axis=1, keepdims=True),
            li_row=grow_ref[pl.ds(col_i, 1), :], bs_row=grow_ref[pl.ds(col_f, 1), :],
            state=(c_ref[ci], n_ref[ci], m_ref[ci]), hidden=[None] * n_sub))
    for j in range(n_sub):
        for ch in chains:
            sub = j if ch["d"] == 0 else n_sub - 1 - j
            rows = slice(sub * M_CHUNK, (sub + 1) * M_CHUNK)
            qk_cols = slice(ch["hh"] * dk, (ch["hh"] + 1) * dk)
            v_cols = slice(ch["hh"] * dv, (ch["hh"] + 1) * dv)
            ch["hidden"][sub], *ch["state"] = _scan_chunk(
                ch["d"], ch["q"][rows, qk_cols], ch["k"][rows, qk_cols], ch["v"][rows, v_cols],
                ch["li_col"][rows], ch["bs_col"][rows], ch["li_row"][:, rows], ch["bs_row"][:, rows], *ch["state"])
    for ch in chains:
        outs[ch["d"]][:, ch["hh"] * dv:(ch["hh"] + 1) * dv] = jnp.concatenate(ch["hidden"], axis=0).astype(bf16)
    for idx, ref in enumerate((c_ref, n_ref, m_ref)):
        ref[...] = jnp.stack([ch["state"][idx] for ch in chains])


def _mlstm_scan(mqk, parts, v_col0, gc_col, gc_row, dims):
    b, s, l, mh, dk, dv = dims
    t = mqk.shape[0]
    rows = SCAN_SUB * M_CHUNK
    assert l % rows == 0 and s % rows == 0
    n_lc, n_sc = l // rows, s // rows
    ctx0 = b * s // rows
    ng2 = gc_col.shape[1]

    def chunk(d, bi, st):
        in_ctx = st < n_lc
        local = jnp.where(in_ctx, st, st - n_lc)
        n_local = jnp.where(in_ctx, n_lc, n_sc)
        pos = local if d == 0 else n_local - 1 - local
        return jnp.where(in_ctx, ctx0 + bi * n_lc, bi * n_sc) + pos

    nh = math.gcd(SCAN_HEADS, mh)
    if v_col0 % (nh * dv):
        nh = 1
    vb = v_col0 // (nh * dv)

    def dir_specs(d):
        return [
            pl.BlockSpec((rows, nh * dk), lambda bi, g, st: (chunk(d, bi, st), g)),
            pl.BlockSpec((rows, nh * dk), lambda bi, g, st: (chunk(d, bi, st), mh // nh + g)),
            pl.BlockSpec((rows, nh * dv), lambda bi, g, st: (chunk(d, bi, st), vb + g)),
            pl.BlockSpec((rows, ng2), lambda bi, g, st: (chunk(d, bi, st), 0)),
            pl.BlockSpec((ng2, rows), lambda bi, g, st: (0, chunk(d, bi, st))),
        ]

    def out_spec(d):
        return pl.BlockSpec((rows, nh * dv), lambda bi, g, st: (chunk(d, bi, st), g))

    state = [pltpu.VMEM((2 * nh, dk, dv), f32), pltpu.VMEM((2 * nh, 1, dk), f32), pltpu.VMEM((2 * nh, 1, 1), f32)]
    args = (mqk, mqk, parts, gc_col, gc_row)
    return pl.pallas_call(
        functools.partial(_scan_kernel, mh=mh, n_sub=SCAN_SUB, n_heads=nh),
        out_shape=[jax.ShapeDtypeStruct((t, mh * dv), bf16)] * 2,
        grid=(b, mh // nh, n_lc + n_sc),
        in_specs=dir_specs(0) + dir_specs(1),
        out_specs=[out_spec(0), out_spec(1)],
        scratch_shapes=state,
        compiler_params=_params("arbitrary", "arbitrary", "arbitrary"),
        name="mlstm_scan",
    )(*args, *args)


def _mout_kernel(hf_ref, hb_ref, o_ref, g_ref, out_ref, *, dv):
    for hh in range(out_ref.shape[1] // dv):
        cols = slice(hh * dv, (hh + 1) * dv)
        hs = hf_ref[:, cols].astype(f32) + hb_ref[:, cols].astype(f32)
        ms = jnp.mean(hs * hs, axis=-1, keepdims=True)
        y = hs * lax.rsqrt(ms + EPS) * g_ref[:, cols]
        out_ref[:, cols] = (y * _sigmoid(o_ref[:, cols].astype(f32))).astype(out_ref.dtype)


def _mlstm_out(h_fwd, h_bwd, parts, o_col0, head_g, rows):
    m_w = h_fwd.shape[1]
    mh, dv = head_g.shape
    tr = ROW_TILE
    wb = math.gcd(MOUT_HEADS * dv, math.gcd(m_w, o_col0))
    ob = o_col0 // wb
    head_spec = pl.BlockSpec((tr, wb), lambda i, g: (i, g))
    return pl.pallas_call(
        functools.partial(_mout_kernel, dv=dv),
        out_shape=jax.ShapeDtypeStruct((rows, m_w), bf16),
        grid=(rows // tr, m_w // wb),
        in_specs=[head_spec, head_spec, pl.BlockSpec((tr, wb), lambda i, g: (i, ob + g)),
                  pl.BlockSpec((1, wb), lambda i, g: (0, g))],
        out_specs=head_spec,
        compiler_params=_params("arbitrary", "arbitrary"),
        name="mlstm_out",
    )(h_fwd, h_bwd, parts, head_g.reshape(1, m_w).astype(f32))


def _merge_kernel(a_ref, wa_ref, m_ref, wm_ref, ga_ref, gb_ref, o_ref):
    ya = jnp.dot(a_ref[...], wa_ref[...], preferred_element_type=f32)
    ym = jnp.dot(m_ref[...], wm_ref[...], preferred_element_type=f32)
    y = _sigmoid(ga_ref[...].astype(f32)) * ya + _sigmoid(gb_ref[...].astype(f32)) * ym
    o_ref[...] = y.astype(o_ref.dtype)


def _merge(a_flat, m_flat, w_a, w_m, layer, parts, ga_col0, rows):
    d = w_a.shape[2]
    tm, tn = 512, min(512, d)
    ga, gb = ga_col0 // tn, (ga_col0 + d) // tn
    return pl.pallas_call(
        _merge_kernel,
        out_shape=jax.ShapeDtypeStruct((rows, d), bf16),
        grid=(d // tn, rows // tm),
        in_specs=[
            pl.BlockSpec((tm, a_flat.shape[1]), lambda j, i: (i, 0)),
            pl.BlockSpec((None, w_a.shape[1], tn), lambda j, i: (layer, 0, j)),
            pl.BlockSpec((tm, m_flat.shape[1]), lambda j, i: (i, 0)),
            pl.BlockSpec((None, w_m.shape[1], tn), lambda j, i: (layer, 0, j)),
            pl.BlockSpec((tm, tn), lambda j, i: (i, ga + j)),
            pl.BlockSpec((tm, tn), lambda j, i: (i, gb + j)),
        ],
        out_specs=pl.BlockSpec((tm, tn), lambda j, i: (i, j)),
        compiler_params=_params("arbitrary", "arbitrary"),
        name="branch_merge",
    )(a_flat, w_a, m_flat, w_m, parts, parts)


def _outproj_kernel(y_ref, w_ref, h_ref, gate_ref, o_ref):
    o_ref[...] = h_ref[...] + gate_ref[0] * jnp.dot(y_ref[...], w_ref[...], preferred_element_type=f32)


def _out_proj(y, w_out, h, mods, layer, seg_of, rows):
    t, d = h.shape
    tm, tn = 512, min(1024, d)
    return pl.pallas_call(
        _outproj_kernel,
        out_shape=jax.ShapeDtypeStruct((t, d), f32),
        grid=(d // tn, rows // tm),
        in_specs=[
            pl.BlockSpec((tm, d), lambda j, i: (i, 0)),
            pl.BlockSpec((None, d, tn), lambda j, i: (layer, 0, j)),
            pl.BlockSpec((tm, tn), lambda j, i: (i, j)),
            pl.BlockSpec((1, 1, tn), lambda j, i: ((layer * 8 + seg_of(i, tm)) * 6 + 2, 0, j)),
        ],
        out_specs=pl.BlockSpec((tm, tn), lambda j, i: (i, j)),
        input_output_aliases={2: 0},
        compiler_params=_params("arbitrary", "arbitrary"),
        name="out_proj",
    )(y, w_out, h, mods)


def _route_kernel(h_ref, g_ref, shift_ref, scale_ref, wr_ref, bias_ref, n_ref, r_ref):
    h = h_ref[...]
    ms = jnp.mean(h * h, axis=-1, keepdims=True)
    n = h * lax.rsqrt(ms + EPS) * g_ref[...] * (1.0 + scale_ref[0]) + shift_ref[0]
    n_ref[...] = n
    logits = lax.dot_general(wr_ref[...], n, (((1,), (1,)), ((), ())), preferred_element_type=f32,
                             precision=lax.Precision.HIGHEST)
    score = _sigmoid(logits)
    sel = score + bias_ref[...]
    sel_e = [sel[e:e + 1] for e in range(N_GROUPS * GROUP_SIZE)]
    score_e = [score[e:e + 1] for e in range(N_GROUPS * GROUP_SIZE)]

    def top2_sum(v):
        best = v[0] + v[1]
        for a in range(GROUP_SIZE):
            for b in range(a + 1, GROUP_SIZE):
                if (a, b) != (0, 1):
                    best = jnp.maximum(best, v[a] + v[b])
        return best

    g_best = top2_sum(sel_e[:GROUP_SIZE])
    grp = jnp.zeros(g_best.shape, jnp.int32)
    for g in range(1, N_GROUPS):
        cand = top2_sum(sel_e[g * GROUP_SIZE:(g + 1) * GROUP_SIZE])
        better = cand > g_best
        grp = jnp.where(better, g, grp)
        g_best = jnp.where(better, cand, g_best)

    def in_group(rows):
        out = []
        for j in range(GROUP_SIZE):
            v = rows[j]
            for g in range(1, N_GROUPS):
                v = jnp.where(grp == g, rows[g * GROUP_SIZE + j], v)
            out.append(v)
        return out

    v, sc = in_group(sel_e), in_group(score_e)
    j1, b1, c1 = jnp.zeros(grp.shape, jnp.int32), v[0], sc[0]
    for j in range(1, GROUP_SIZE):
        better = v[j] > b1
        j1, b1, c1 = jnp.where(better, j, j1), jnp.where(better, v[j], b1), jnp.where(better, sc[j], c1)
    j2 = jnp.full(grp.shape, -1, jnp.int32)
    b2, c2 = jnp.full(b1.shape, -jnp.inf, f32), jnp.zeros(b1.shape, f32)
    for j in range(GROUP_SIZE):
        better = (j1 != j) & ((v[j] > b2) | (j2 < 0))
        j2, b2, c2 = jnp.where(better, j, j2), jnp.where(better, v[j], b2), jnp.where(better, sc[j], c2)
    tot = c1 + c2
    w1, w2 = c1 / tot, c2 / tot
    first_low = j1 < j2
    lo, hi = jnp.minimum(j1, j2), jnp.maximum(j1, j2)
    pair_in_group = jnp.zeros(lo.shape, jnp.int32)
    for idx, (p_lo, p_hi) in enumerate(PAIR_ORDER):
        pair_in_group = jnp.where((lo == p_lo) & (hi == p_hi), idx, pair_in_group)
    r_ref[...] = jnp.zeros(r_ref.shape, f32)
    r_ref[0:1, :] = (grp * 6 + pair_in_group).astype(f32)
    r_ref[1:2, :] = jnp.where(first_low, w1, w2)
    r_ref[2:3, :] = jnp.where(first_low, w2, w1)


def _norm_route(h, g, mods, layer, seg_of, w_router_t, bias, rows):
    t, d = h.shape
    e = w_router_t.shape[0]
    tr = ROW_TILE

    def mod_spec(k):
        return pl.BlockSpec((1, 1, d), lambda i: ((layer * 8 + seg_of(i, tr)) * 6 + k, 0, 0))

    return pl.pallas_call(
        _route_kernel,
        out_shape=[jax.ShapeDtypeStruct((rows, d), f32), jax.ShapeDtypeStruct((8, rows), f32)],
        grid=(rows // tr,),
        in_specs=[pl.BlockSpec((tr, d), lambda i: (i, 0)), pl.BlockSpec((1, d), lambda i: (0, 0)),
                  mod_spec(3), mod_spec(4), pl.BlockSpec((e, d), lambda i: (0, 0)),
                  pl.BlockSpec((e, 1), lambda i: (0, 0))],
        out_specs=[pl.BlockSpec((tr, d), lambda i: (i, 0)), pl.BlockSpec((8, tr), lambda i: (0, i))],
        compiler_params=_params("arbitrary"),
        name="norm_route",
    )(h, g.reshape(1, d), mods, mods, w_router_t, bias.reshape(e, 1).astype(f32))


def _moe_kernel(lo_ref, hi_ref, nv_ref, tok_ref, x_hbm, w_ref, wg_lo, wu_lo, wd_lo, wg_hi, wu_hi, wd_hi,
                out_hbm, xbuf, ybuf, sem_in, sem_out):
    blk = pl.program_id(0)
    slot = blk % 2

    def row_in(b, r):
        return pltpu.make_async_copy(x_hbm.at[pl.ds(tok_ref[b * MOE_BLOCK + r], 1)],
                                     xbuf.at[b % 2, pl.ds(r, 1)], sem_in.at[b % 2])

    def row_out(b, r):
        return pltpu.make_async_copy(ybuf.at[b % 2, pl.ds(r, 1)],
                                     out_hbm.at[pl.ds(tok_ref[b * MOE_BLOCK + r], 1)], sem_out.at[b % 2])

    def for_rows(fn):
        def body(r, c):
            fn(r)
            return c

        lax.fori_loop(0, MOE_BLOCK, body, 0, unroll=8)

    def start_in(b):
        for r in range(MOE_BLOCK):
            row_in(b, r).start()

    def start_out(b):
        nv = nv_ref[b]

        @pl.when(nv == MOE_BLOCK)
        def _():
            for r in range(MOE_BLOCK):
                row_out(b, r).start()

        @pl.when(nv < MOE_BLOCK)
        def _():
            for_rows(lambda r: pl.when(r < nv)(lambda: row_out(b, r).start()))

    def wait_in(b):
        pltpu.make_async_copy(x_hbm.at[pl.ds(0, MOE_BLOCK)], xbuf.at[b % 2], sem_in.at[b % 2]).wait()

    def wait_out(b):
        nv = nv_ref[b]
        p = 1
        while p <= MOE_BLOCK:
            rows = pl.ds(0, p)
            pl.when((nv & p) != 0)(
                pltpu.make_async_copy(ybuf.at[b % 2, rows], out_hbm.at[rows], sem_out.at[b % 2]).wait)
            p *= 2

    def has_rows(b):
        n_blk = pl.num_programs(0)
        return jnp.logical_and(b < n_blk, nv_ref[jnp.minimum(b, n_blk - 1)] > 0)

    @pl.when(nv_ref[blk] > 0)
    def _():
        @pl.when(blk == 0)
        def _():
            start_in(blk)

        wait_in(blk)

        x = xbuf[slot].astype(bf16)
        eye = (lax.broadcasted_iota(jnp.int32, (MOE_BLOCK, MOE_BLOCK), 0)
               == lax.broadcasted_iota(jnp.int32, (MOE_BLOCK, MOE_BLOCK), 1))
        w = w_ref[0]

        def expert(wg, wu, wd, w_row):
            a = jnp.dot(x, wg[0], preferred_element_type=f32)
            u = jnp.dot(x, wu[0], preferred_element_type=f32)
            y = jnp.dot((a * _sigmoid(a) * u).astype(bf16), wd[0], preferred_element_type=f32)
            return y * jnp.sum(jnp.where(eye, w_row, 0.0), axis=1, keepdims=True)

        ybuf[slot] = expert(wg_lo, wu_lo, wd_lo, w[0:1]) + expert(wg_hi, wu_hi, wd_hi, w[1:2])
        start_in(blk + 1)

        @pl.when(blk > 0)
        def _():
            wait_out(blk - 1)

        start_out(blk)

        @pl.when(jnp.logical_not(has_rows(blk + 1)))
        def _():
            wait_out(blk)

    @pl.when(jnp.logical_and(nv_ref[blk] == 0, nv_ref[jnp.maximum(blk - 1, 0)] > 0))
    def _():
        wait_in(blk)


def _moe(n2, route, w_gate, w_up, w_down, layer, n_tok):
    d = n2.shape[1]
    dff = w_gate.shape[-1]
    blk = MOE_BLOCK
    n_blk = -(-n_tok // blk) + N_PAIRS
    n_pad = n_blk * blk
    pair = route[0, :n_tok].astype(jnp.int32)
    onehot = (pair[:, None] == jnp.arange(N_PAIRS, dtype=jnp.int32)[None, :]).astype(jnp.int32)
    csum = jnp.cumsum(onehot, axis=0)
    rank = jnp.sum(csum * onehot, axis=1) - 1
    counts = csum[-1]
    padded = (counts + blk - 1) // blk * blk
    pad_end = jnp.cumsum(padded)
    pad_start = pad_end - padded
    dest = pad_start[pair] + rank
    buf_tok = jnp.zeros((n_pad,), jnp.int32).at[dest].set(jnp.arange(n_tok, dtype=jnp.int32))
    wts = jnp.zeros((2, n_pad), f32).at[:, dest].set(route[1:3, :n_tok])
    wts = wts.reshape(2, n_blk, blk).transpose(1, 0, 2)
    blk_start = jnp.arange(n_blk, dtype=jnp.int32) * blk
    seg = jnp.minimum(jnp.searchsorted(pad_end, blk_start, side="right"), N_PAIRS - 1).astype(jnp.int32)
    nv = jnp.clip(counts[seg] - (blk_start - pad_start[seg]), 0, blk).astype(jnp.int32)
    pair_tbl = jnp.asarray(PAIR_ORDER, jnp.int32)
    e_lo = (seg // len(PAIR_ORDER)) * GROUP_SIZE + pair_tbl[seg % len(PAIR_ORDER), 0]
    e_hi = (seg // len(PAIR_ORDER)) * GROUP_SIZE + pair_tbl[seg % len(PAIR_ORDER), 1]

    def w_in(sel):
        return pl.BlockSpec((None, 1, d, dff), lambda i, lo, hi, nvr, tok: (layer, (lo, hi)[sel][i], 0, 0),
                            pipeline_mode=pl.Buffered(1))

    def w_out(sel):
        return pl.BlockSpec((None, 1, dff, d), lambda i, lo, hi, nvr, tok: (layer, (lo, hi)[sel][i], 0, 0),
                            pipeline_mode=pl.Buffered(1))

    return pl.pallas_call(
        _moe_kernel,
        out_shape=jax.ShapeDtypeStruct((n_tok, d), f32),
        grid_spec=pltpu.PrefetchScalarGridSpec(
            num_scalar_prefetch=4,
            grid=(n_blk,),
            in_specs=[pl.BlockSpec(memory_space=pl.ANY),
                      pl.BlockSpec((1, 2, blk), lambda i, lo, hi, nvr, tok: (i, 0, 0)),
                      w_in(0), w_in(0), w_out(0), w_in(1), w_in(1), w_out(1)],
            out_specs=pl.BlockSpec(memory_space=pl.ANY),
            scratch_shapes=[pltpu.VMEM((2, blk, d), f32), pltpu.VMEM((2, blk, d), f32),
                            pltpu.SemaphoreType.DMA((2,)), pltpu.SemaphoreType.DMA((2,))],
        ),
        compiler_params=_params("arbitrary"),
        name="moe",
    )(e_lo, e_hi, nv, buf_tok, n2, wts, w_gate, w_up, w_down, w_gate, w_up, w_down)


def _rope_tables(n_rows, hd, pad_rows):
    n_freq = hd // 4
    inv_freq = ROPE_THETA ** (-jnp.arange(n_freq, dtype=f32) / n_freq)
    pos = jnp.arange(n_rows * GRID_W)
    ang_row = (pos // GRID_W).astype(f32)[:, None] * inv_freq
    ang_col = (pos % GRID_W).astype(f32)[:, None] * inv_freq
    ang = jnp.concatenate([ang_row, ang_row, ang_col, ang_col], axis=1)
    sign = jnp.tile(jnp.concatenate([-jnp.ones((n_freq,), f32), jnp.ones((n_freq,), f32)]), 2)
    cos_t = jnp.concatenate([jnp.cos(ang), jnp.ones((pad_rows, hd), f32)], axis=0)
    sin_t = jnp.concatenate([jnp.sin(ang) * sign, jnp.zeros((pad_rows, hd), f32)], axis=0)
    return cos_t, sin_t


def kernel(x, c, ctx, c_ctx, w_ada, b_ada, g_norm1, g_norm2, w_in, q_norm_g, k_norm_g, diff_lambda, attn_head_g,
           conv_w, conv_b, mlstm_gate_b, mlstm_head_g, w_br_attn, w_br_mlstm, w_out, w_router, router_bias,
           w_gate, w_up, w_down):
    b, s, d = x.shape
    l = ctx.shape[1]
    depth = w_ada.shape[0]
    hd = q_norm_g.shape[-1]
    a_w = w_br_attn.shape[1]
    heads = a_w // (2 * hd)
    mh, dv = mlstm_head_g.shape[-2:]
    qk_w = conv_w.shape[-1] // 2
    dk = qk_w // mh
    m_w = mh * dv
    ng = 4 * mh
    assert w_router.shape[1] == N_GROUPS * GROUP_SIZE
    assert s % ROW_TILE == 0 and l % ROW_TILE == 0 and s % GRID_W == 0 and b + 1 <= 8
    t_x, t_all = b * s, b * s + b * l
    gate_col = 3 * a_w + 2 * qk_w + 2 * m_w
    mq_col, mv_col, mo_col = 3 * a_w, 3 * a_w + 2 * qk_w, 3 * a_w + 2 * qk_w + m_w

    def seg_of(i, tile):
        return jnp.where(i < t_x // tile, i // (s // tile), b)

    w_in_bf = w_in.astype(bf16)
    w_g = w_in_bf[:, :, gate_col:gate_col + ng]
    w_bg = w_in_bf[:, :, gate_col + ng:]
    w_a, w_m, w_o = w_br_attn.astype(bf16), w_br_mlstm.astype(bf16), w_out.astype(bf16)
    w_eg, w_eu, w_ed = w_gate.astype(bf16), w_up.astype(bf16), w_down.astype(bf16)
    cv = jnp.zeros((8, d), f32).at[:b].set(c).at[b].set(c_ctx)
    mods = _ada(cv, b + 1, w_ada, b_ada).reshape(depth * 8 * 6, 1, d)
    cos_t, sin_t = _rope_tables(s // GRID_W, hd, ROW_TILE)
    h = jnp.concatenate([x.reshape(t_x, d), ctx.reshape(b * l, d)], axis=0)
    f = None
    for layer in range(depth):
        need_ctx = layer < depth - 1
        rows = t_all if need_ctx else t_x
        lam_init = 0.8 - 0.6 * math.exp(-0.3 * layer)
        lv = diff_lambda[layer].astype(f32)
        lam = (jnp.exp(jnp.sum(lv[0] * lv[1])) - jnp.exp(jnp.sum(lv[2] * lv[3])) + lam_init).reshape(1, 1)
        if f is None:
            (n1,) = _resid_norm(h, t_all, seg_of, mods, norm=(g_norm1[layer], layer, 0, 1))
        else:
            h, n1 = _resid_norm(h, t_all, seg_of, mods, resid=(f, layer - 1, 5), norm=(g_norm1[layer], layer, 0, 1))
        parts = _matmul(n1, w_in_bf, layer, bf16, n=gate_col)
        branch_gates = _matmul(n1, w_bg, layer, bf16)
        g_col, c_col = _gates(n1, w_g, layer, mlstm_gate_b[layer], mh)

        qk_gain = jnp.concatenate([jnp.tile(q_norm_g[layer].astype(f32), 2 * heads) * hd ** -0.5,
                                   jnp.tile(k_norm_g[layer].astype(f32), 2 * heads)]).reshape(1, 2 * a_w)
        qk2 = _qk_prep(parts, qk_gain, cos_t, sin_t, 2 * a_w, t_x // ROW_TILE, s // ROW_TILE, hd)
        out_gain = (attn_head_g[layer].astype(f32) * (1.0 - lam_init)).reshape(1, 2 * hd)
        bound = (1.01 * hd ** 0.5 * jnp.max(jnp.abs(q_norm_g[layer])) * jnp.max(jnp.abs(k_norm_g[layer])))
        a_flat = _attention(qk2, parts, lam, bound.astype(f32).reshape(1, 1), out_gain, (b, s, l, heads, hd), need_ctx)

        post = jnp.concatenate([jnp.ones((qk_w,), f32), jnp.full((qk_w,), dk ** -0.5, f32)]).reshape(1, 2 * qk_w)
        mqk = _mlstm_prep(parts, mq_col, conv_w[layer].astype(f32), conv_b[layer].reshape(1, -1).astype(f32), post,
                          (b, s, l))
        gc_col = jnp.concatenate([g_col, c_col], axis=1)
        h_fwd, h_bwd = _mlstm_scan(mqk, parts, mv_col, gc_col, gc_col.T, (b, s, l, mh, dk, dv))
        m_flat = _mlstm_out(h_fwd, h_bwd, parts, mo_col, mlstm_head_g[layer], rows)

        y = _merge(a_flat, m_flat, w_a, w_m, layer, branch_gates, 0, rows)
        h = _out_proj(y, w_o, h, mods, layer, seg_of, rows)
        n2, route = _norm_route(h, g_norm2[layer], mods, layer, seg_of, w_router.T.astype(f32), router_bias, rows)
        f = _moe(n2, route, w_eg, w_eu, w_ed, layer, rows)
    (h,) = _resid_norm(h, t_x, seg_of, mods, resid=(f, depth - 1, 5))
    return h.reshape(b, s, d)
```
